```python
import math
import jax, jax.numpy as jnp
from jax import lax
import numpy as np

D_MODEL = 1024
BATCH = 16
SEQ = 2048
DEPTH = 2

MEM_LEN = 256
ATTN_HEADS = 8
ATTN_HEAD_DIM = 64
D_ATTN = ATTN_HEADS * ATTN_HEAD_DIM
D_SSM = D_MODEL // 4
SSM_GROUP = 16
SSM_GROUPS = D_SSM // SSM_GROUP
SSM_STATE = 64
D_LRU = D_MODEL // 4
LRU_HEADS = 4
LRU_HEAD_DIM = D_LRU // LRU_HEADS
CONV_WIDTH = 4
LRU_C = 8.0
D_MIX = D_ATTN + D_SSM + D_LRU
D_IN = 3 * D_ATTN + D_SSM + 2 * D_LRU
SPLITS = (D_ATTN, 2 * D_ATTN, 3 * D_ATTN, 3 * D_ATTN + D_SSM, 3 * D_ATTN + D_SSM + D_LRU)
MOBA_BLOCK = 256
MOBA_TOPK = 3
MOBA_Q_CHUNK = 16
MEM_HEADS = 4
MEM_HEAD_DIM = D_MODEL // MEM_HEADS
N_GROUPS = 4
EXPERTS_PER_GROUP = 4
N_EXPERTS = N_GROUPS * EXPERTS_PER_GROUP
EXPERT_TOPK = 2
D_EXPERT = 256
ALPHA = (2.0 * DEPTH) ** 0.25
BETA = (8.0 * DEPTH) ** -0.25
LN_EPS = 1e-5
RMS_EPS = 1e-6
NEG_INF = -1e30

kernel_name = "hymba_moba_s5_rglru_hmoe_deepnorm"


def layer_norm(x, g, b):
    xf = x.astype(jnp.float32)
    mu = jnp.mean(xf, axis=-1, keepdims=True)
    var = jnp.mean(jnp.square(xf - mu), axis=-1, keepdims=True)
    y = (xf - mu) * lax.rsqrt(var + LN_EPS) * g.astype(jnp.float32) + b.astype(jnp.float32)
    return y.astype(x.dtype)


def rms_normalize(x):
    xf = x.astype(jnp.float32)
    return (xf * lax.rsqrt(jnp.mean(xf * xf, axis=-1, keepdims=True) + RMS_EPS)).astype(x.dtype)


def alibi_slopes(n_heads):
    return jnp.asarray([2.0 ** (-8.0 * (h + 1) / n_heads) for h in range(n_heads)], jnp.float32)


def moba_attention(q, k, v):
    bsz, s, nh, dh = q.shape
    f32 = jnp.float32
    nb = -(-s // MOBA_BLOCK)
    sp = nb * MOBA_BLOCK
    kk = min(MOBA_TOPK, nb)
    nc = sp // MOBA_Q_CHUNK
    pad = ((0, 0), (0, sp - s), (0, 0), (0, 0))
    q, k, v = [jnp.pad(t, pad).transpose(0, 2, 1, 3) for t in (q, k, v)]
    kb = k.reshape(bsz, nh, nb, MOBA_BLOCK, dh)
    vb = v.reshape(bsz, nh, nb, MOBA_BLOCK, dh)
    kmean = jnp.mean(kb.astype(f32), axis=3)
    gate = jnp.einsum('bhtd,bhnd->bhtn', q.astype(f32), kmean)
    qblk = jnp.arange(sp) // MOBA_BLOCK
    fully_past = jnp.arange(nb)[None, :] < qblk[:, None]
    gate = jnp.where(fully_past, gate, NEG_INF)
    _, sel = lax.top_k(gate, kk)
    slopes = alibi_slopes(nh)
    scale = dh ** -0.5
    q_chunks = q.reshape(bsz, nh, nc, MOBA_Q_CHUNK, dh).transpose(2, 0, 1, 3, 4)
    sel_chunks = sel.reshape(bsz, nh, nc, MOBA_Q_CHUNK, kk).transpose(2, 0, 1, 3, 4)
    gather_blocks = jax.vmap(jax.vmap(lambda blocks, idx: blocks[idx]))
    offs = jnp.arange(MOBA_BLOCK)

    def attend_chunk(args):
        c, q_c, sel_c = args
        t = c * MOBA_Q_CHUNK + jnp.arange(MOBA_Q_CHUNK)
        own = (c * MOBA_Q_CHUNK) // MOBA_BLOCK
        k_sel = gather_blocks(kb, sel_c)
        v_sel = gather_blocks(vb, sel_c)
        s_sel = jnp.einsum('bhqd,bhqjsd->bhqjs', q_c, k_sel).astype(f32) * scale
        dist_sel = (t[:, None, None] - (sel_c[..., None] * MOBA_BLOCK + offs)).astype(f32)
        s_sel = s_sel - slopes[None, :, None, None, None] * dist_sel
        sel_ok = jnp.arange(kk) < own
        s_sel = jnp.where(sel_ok[:, None], s_sel, NEG_INF)
        k_own = lax.dynamic_index_in_dim(kb, own, axis=2, keepdims=False)
        v_own = lax.dynamic_index_in_dim(vb, own, axis=2, keepdims=False)
        s_own = jnp.einsum('bhqd,bhsd->bhqs', q_c, k_own).astype(f32) * scale
        dist_own = (t[:, None] - (own * MOBA_BLOCK + offs)[None, :]).astype(f32)
        s_own = jnp.where(dist_own >= 0, s_own - slopes[:, None, None] * dist_own, NEG_INF)
        scores = jnp.concatenate([s_sel.reshape(bsz, nh, MOBA_Q_CHUNK, kk * MOBA_BLOCK), s_own], axis=-1)
        p = jax.nn.softmax(scores, axis=-1).astype(v_sel.dtype)
        p_sel = p[..., :kk * MOBA_BLOCK].reshape(bsz, nh, MOBA_Q_CHUNK, kk, MOBA_BLOCK)
        p_own = p[..., kk * MOBA_BLOCK:]
        return (jnp.einsum('bhqjs,bhqjsd->bhqd', p_sel, v_sel)
                + jnp.einsum('bhqs,bhsd->bhqd', p_own, v_own))

    out = lax.map(attend_chunk, (jnp.arange(nc), q_chunks, sel_chunks))
    out = out.transpose(1, 0, 3, 2, 4).reshape(bsz, sp, nh * dh)
    return out[:, :s]


def s5_mixer(u, a_re, a_im, b_re, b_im, c_re, c_im, d_skip, log_dt, w_glu, b_glu):
    f32 = jnp.float32
    bsz, s, _ = u.shape
    uf = u.astype(f32).reshape(bsz, s, SSM_GROUPS, SSM_GROUP)
    dt = jnp.exp(log_dt.astype(f32))[:, None]
    are, aim = a_re.astype(f32), a_im.astype(f32)
    mag = jnp.exp(dt * are)
    ab_re, ab_im = mag * jnp.cos(dt * aim), mag * jnp.sin(dt * aim)
    den = are * are + aim * aim
    nr, ni = ab_re - 1.0, ab_im
    f_re = (nr * are + ni * aim) / den
    f_im = (ni * are - nr * aim) / den
    br, bi = b_re.astype(f32), b_im.astype(f32)
    bb_re = f_re[..., None] * br - f_im[..., None] * bi
    bb_im = f_re[..., None] * bi + f_im[..., None] * br
    bu_re = jnp.einsum('gph,bsgh->bsgp', bb_re, uf)
    bu_im = jnp.einsum('gph,bsgh->bsgp', bb_im, uf)
    at_re = jnp.broadcast_to(ab_re, bu_re.shape)
    at_im = jnp.broadcast_to(ab_im, bu_im.shape)

    def combine(e1, e2):
        a1r, a1i, b1r, b1i = e1
        a2r, a2i, b2r, b2i = e2
        return (a2r * a1r - a2i * a1i, a2r * a1i + a2i * a1r,
                a2r * b1r - a2i * b1i + b2r, a2r * b1i + a2i * b1r + b2i)

    _, _, xr, xi = lax.associative_scan(combine, (at_re, at_im, bu_re, bu_im), axis=1)
    y = (jnp.einsum('ghp,bsgp->bsgh', c_re.astype(f32), xr)
         - jnp.einsum('ghp,bsgp->bsgh', c_im.astype(f32), xi)).reshape(bsz, s, D_SSM)
    y = jax.nn.gelu(y + d_skip.astype(f32) * u.astype(f32))
    y = y * jax.nn.sigmoid(y @ w_glu.astype(f32) + b_glu.astype(f32))
    return y.astype(u.dtype)


def rglru_mixer(xl, gl, conv_w, conv_b, w_a, b_a, w_x, b_x, lam):
    f32 = jnp.float32
    bsz, s, _ = xl.shape
    xc = lax.conv_general_dilated(xl.astype(f32), conv_w.astype(f32)[:, None, :], window_strides=(1,),
                                  padding=[(CONV_WIDTH - 1, 0)], dimension_numbers=('NWC', 'WIO', 'NWC'),
                                  feature_group_count=D_LRU) + conv_b.astype(f32)
    xh = xc.reshape(bsz, s, LRU_HEADS, LRU_HEAD_DIM)
    r = jax.nn.sigmoid(jnp.einsum('bshi,hij->bshj', xh, w_a.astype(f32)).reshape(bsz, s, D_LRU) + b_a.astype(f32))
    i = jax.nn.sigmoid(jnp.einsum('bshi,hij->bshj', xh, w_x.astype(f32)).reshape(bsz, s, D_LRU) + b_x.astype(f32))
    log_a = -LRU_C * r * jax.nn.softplus(-lam.astype(f32))
    a = jnp.exp(log_a)
    bterm = jnp.sqrt(-jnp.expm1(2.0 * log_a)) * (i * xc)
    _, hseq = lax.associative_scan(lambda e1, e2: (e1[0] * e2[0], e2[0] * e1[1] + e2[1]), (a, bterm), axis=1)
    return (hseq * jax.nn.gelu(gl.astype(f32))).astype(xl.dtype)


def hybrid_mixer(h, w_in, mix_g, w_out, a_re, a_im, b_re, b_im, c_re, c_im, d_skip, log_dt, w_glu, b_glu,
                 conv_w, conv_b, w_a, b_a, w_x, b_x, lam):
    bsz, s, _ = h.shape
    z = h @ w_in
    q, k, v, u, xl, gl = jnp.split(z, SPLITS, axis=-1)
    hd = (bsz, s, ATTN_HEADS, ATTN_HEAD_DIM)
    y_attn = moba_attention(q.reshape(hd), k.reshape(hd), v.reshape(hd))
    y_ssm = s5_mixer(u, a_re, a_im, b_re, b_im, c_re, c_im, d_skip, log_dt, w_glu, b_glu)
    y_lru = rglru_mixer(xl, gl, conv_w, conv_b, w_a, b_a, w_x, b_x, lam)
    y = jnp.concatenate([rms_normalize(y_attn), rms_normalize(y_ssm), rms_normalize(y_lru)], axis=-1)
    return (y * mix_g) @ w_out


def memory_cross_attention(h, mem, wq, wk, wv, wo):
    bsz, s, _ = h.shape
    m = mem.shape[1]
    q = (h @ wq).reshape(bsz, s, MEM_HEADS, MEM_HEAD_DIM)
    k = (mem @ wk).reshape(bsz, m, MEM_HEADS, MEM_HEAD_DIM)
    v = (mem @ wv).reshape(bsz, m, MEM_HEADS, MEM_HEAD_DIM)
    sc = jnp.einsum('bqhd,bkhd->bhqk', q, k).astype(jnp.float32) * (MEM_HEAD_DIM ** -0.5)
    p = jax.nn.softmax(sc, axis=-1).astype(v.dtype)
    o = jnp.einsum('bhqk,bkhd->bqhd', p, v).reshape(bsz, s, D_MODEL)
    return o @ wo


def hier_moe(h, wr_g, br_g, wr_e, br_e, w_gate, w_up, w_down):
    f32 = jnp.float32
    bsz, s, d = h.shape
    t = h.reshape(bsz * s, d)
    g_logits = (t @ wr_g).astype(f32) + br_g.astype(f32)
    g_prob = jax.nn.softmax(g_logits, axis=-1)
    g_onehot = jax.nn.one_hot(jnp.argmax(g_logits, axis=-1), N_GROUPS, dtype=f32)
    g_w = jnp.sum(g_prob * g_onehot, axis=-1, keepdims=True)
    e_logits = ((t @ wr_e).astype(f32) + br_e.astype(f32)).reshape(-1, N_GROUPS, EXPERTS_PER_GROUP)
    e_in_group = jnp.einsum('ng,nge->ne', g_onehot, e_logits)
    e_prob = jax.nn.softmax(e_in_group, axis=-1)
    top_p, top_i = lax.top_k(e_prob, EXPERT_TOPK)
    top_p = top_p / jnp.sum(top_p, axis=-1, keepdims=True)
    local = jnp.sum(jax.nn.one_hot(top_i, EXPERTS_PER_GROUP, dtype=f32) * top_p[..., None], axis=1)
    combine = (g_onehot[:, :, None] * local[:, None, :] * g_w[:, :, None]).reshape(-1, N_EXPERTS).astype(t.dtype)
    out = jnp.zeros_like(t)
    for e in range(N_EXPERTS):
        he = jax.nn.silu(t @ w_gate[e]) * (t @ w_up[e])
        out = out + combine[:, e:e + 1] * (he @ w_down[e])
    return out.reshape(bsz, s, d)


def setup_inputs(seed: int = 0) -> dict:
    key = jax.random.key(seed)
    keys = jax.random.split(key, 64)
    counter = [0]

    def nk():
        kk_ = keys[counter[0]]
        counter[0] += 1
        return kk_

    def nrm(shape, scale):
        return scale * jax.random.normal(nk(), shape, jnp.float32)

    def gain(shape):
        return 1.0 + nrm(shape, 0.02)

    L, d = DEPTH, D_MODEL
    G, P, H = SSM_GROUPS, SSM_STATE, SSM_GROUP
    inp = {}
    inp['x'] = nrm((BATCH, SEQ, d), 1.0)
    inp['mem'] = nrm((BATCH, MEM_LEN, d), 1.0)
    inp['ln0_g'] = gain((d,))
    inp['ln0_b'] = nrm((d,), 0.02)
    inp['w_in'] = nrm((L, d, D_IN), d ** -0.5)
    inp['mix_g'] = gain((L, D_MIX))
    inp['w_out'] = nrm((L, D_MIX, d), BETA * D_MIX ** -0.5)
    inp['ssm_a_re'] = -0.5 + nrm((L, G, P), 0.01)
    inp['ssm_a_im'] = math.pi * jnp.arange(P, dtype=jnp.float32) + nrm((L, G, P), 0.01)
    inp['ssm_b_re'] = nrm((L, G, P, H), (2.0 * H) ** -0.5)
    inp['ssm_b_im'] = nrm((L, G, P, H), (2.0 * H) ** -0.5)
    inp['ssm_c_re'] = nrm((L, G, H, P), P ** -0.5)
    inp['ssm_c_im'] = nrm((L, G, H, P), P ** -0.5)
    inp['ssm_d'] = nrm((L, D_SSM), 1.0)
    inp['ssm_log_dt'] = jax.random.uniform(nk(), (L, G), jnp.float32, math.log(1e-3), math.log(1e-1))
    inp['ssm_w_glu'] = nrm((L, D_SSM, D_SSM), D_SSM ** -0.5)
    inp['ssm_b_glu'] = nrm((L, D_SSM), 0.01)
    inp['lru_conv_w'] = nrm((L, CONV_WIDTH, D_LRU), CONV_WIDTH ** -0.5)
    inp['lru_conv_b'] = nrm((L, D_LRU), 0.01)
    inp['lru_w_a'] = nrm((L, LRU_HEADS, LRU_HEAD_DIM, LRU_HEAD_DIM), LRU_HEAD_DIM ** -0.5)
    inp['lru_b_a'] = nrm((L, D_LRU), 0.01)
    inp['lru_w_x'] = nrm((L, LRU_HEADS, LRU_HEAD_DIM, LRU_HEAD_DIM), LRU_HEAD_DIM ** -0.5)
    inp['lru_b_x'] = nrm((L, D_LRU), 0.01)
    a_pow_c = jax.random.uniform(nk(), (L, D_LRU), jnp.float32, 0.9, 0.999)
    a_base = a_pow_c ** (1.0 / LRU_C)
    inp['lru_lam'] = jnp.log(a_base) - jnp.log1p(-a_base)
    inp['ln1_g'] = gain((L, d))
    inp['ln1_b'] = nrm((L, d), 0.02)
    inp['mem_wq'] = nrm((L, d, d), d ** -0.5)
    inp['mem_wk'] = nrm((L, d, d), d ** -0.5)
    inp['mem_wv'] = nrm((L, d, d), d ** -0.5)
    inp['mem_wo'] = nrm((L, d, d), BETA * d ** -0.5)
    inp['ln2_g'] = gain((L, d))
    inp['ln2_b'] = nrm((L, d), 0.02)
    inp['moe_wr_g'] = nrm((L, d, N_GROUPS), d ** -0.5)
    inp['moe_br_g'] = nrm((L, N_GROUPS), 0.01)
    inp['moe_wr_e'] = nrm((L, d, N_EXPERTS), d ** -0.5)
    inp['moe_br_e'] = nrm((L, N_EXPERTS), 0.01)
    inp['moe_w_gate'] = nrm((L, N_EXPERTS, d, D_EXPERT), d ** -0.5)
    inp['moe_w_up'] = nrm((L, N_EXPERTS, d, D_EXPERT), d ** -0.5)
    inp['moe_w_down'] = nrm((L, N_EXPERTS, D_EXPERT, d), BETA * D_EXPERT ** -0.5)
    inp['ln3_g'] = gain((L, d))
    inp['ln3_b'] = nrm((L, d), 0.02)
    return inp


def reference(x, mem, ln0_g, ln0_b, w_in, mix_g, w_out,
              ssm_a_re, ssm_a_im, ssm_b_re, ssm_b_im, ssm_c_re, ssm_c_im, ssm_d, ssm_log_dt, ssm_w_glu, ssm_b_glu,
              lru_conv_w, lru_conv_b, lru_w_a, lru_b_a, lru_w_x, lru_b_x, lru_lam,
              ln1_g, ln1_b, mem_wq, mem_wk, mem_wv, mem_wo, ln2_g, ln2_b,
              moe_wr_g, moe_br_g, moe_wr_e, moe_br_e, moe_w_gate, moe_w_up, moe_w_down, ln3_g, ln3_b):
    h = layer_norm(x, ln0_g, ln0_b)
    for l in range(DEPTH):
        mix = hybrid_mixer(h, w_in[l], mix_g[l], w_out[l],
                           ssm_a_re[l], ssm_a_im[l], ssm_b_re[l], ssm_b_im[l], ssm_c_re[l], ssm_c_im[l],
                           ssm_d[l], ssm_log_dt[l], ssm_w_glu[l], ssm_b_glu[l],
                           lru_conv_w[l], lru_conv_b[l], lru_w_a[l], lru_b_a[l], lru_w_x[l], lru_b_x[l], lru_lam[l])
        h = layer_norm(ALPHA * h + mix, ln1_g[l], ln1_b[l])
        cross = memory_cross_attention(h, mem, mem_wq[l], mem_wk[l], mem_wv[l], mem_wo[l])
        h = layer_norm(ALPHA * h + cross, ln2_g[l], ln2_b[l])
        ffn = hier_moe(h, moe_wr_g[l], moe_br_g[l], moe_wr_e[l], moe_br_e[l],
                       moe_w_gate[l], moe_w_up[l], moe_w_down[l])
        h = layer_norm(ALPHA * h + ffn, ln3_g[l], ln3_b[l])
    return h
```

```python
import functools
import math

import jax
import jax.numpy as jnp
from jax import lax
from jax.experimental import pallas as pl
from jax.experimental.pallas import tpu as pltpu

F32 = jnp.float32
BF16 = jnp.bfloat16

D_ATTN = 512
D_SSM = 256
D_LRU = 256
SSM_GROUPS = 16
SSM_GROUP = 16
SSM_STATE = 64
D_STATE = SSM_GROUPS * SSM_STATE
LRU_HEADS = 4
CONV_WIDTH = 4
LRU_C = 8.0
ATTN_HEADS = 8
ATTN_HEAD_DIM = 64
HEADS_PER_STEP = 4
MOBA_BLOCK = 256
MOBA_TOPK = 3
MEM_HEADS = 4
N_GROUPS = 4
EXPERTS_PER_GROUP = 4
N_EXPERTS = 16
DEPTH = 2
ALPHA = (2.0 * DEPTH) ** 0.25
LN_EPS = 1e-5
RMS_EPS = 1e-6
NEG_INF = -1e30

SUBLANES = 8
LANES = 128
VMEM_LIMIT = 48 * 1024 * 1024
ROW_TILE = 512
SCAN_CHUNK = 512


def _params(*sem):
    return pltpu.CompilerParams(dimension_semantics=sem, vmem_limit_bytes=VMEM_LIMIT)


def _const_spec(shape):
    zeros = (0,) * len(shape)
    return pl.BlockSpec(shape, lambda *_: zeros)


def _layer_norm(x, g, b):
    mu = jnp.mean(x, axis=-1, keepdims=True)
    xc = x - mu
    var = jnp.mean(xc * xc, axis=-1, keepdims=True)
    return xc * lax.rsqrt(var + LN_EPS) * g + b


def _rms(x):
    return x * lax.rsqrt(jnp.mean(x * x, axis=-1, keepdims=True) + RMS_EPS)


def _dot(a, b):
    return jnp.dot(a, b, preferred_element_type=F32)


def _dot_nt(a, b):
    return lax.dot_general(a, b, (((1,), (1,)), ((), ())), preferred_element_type=F32)


def _split_bf16(x):
    hi = x.astype(BF16)
    lo = (x - hi.astype(F32)).astype(BF16)
    return hi, lo


def _dot3(a, b, dot):
    ah, al = _split_bf16(a)
    bh, bl = _split_bf16(b)
    return dot(ah, bh) + dot(ah, bl) + dot(al, bh)


def _ln_kernel(x_ref, g_ref, b_ref, o_ref):
    o_ref[...] = _layer_norm(x_ref[...], g_ref[...], b_ref[...])


def _ln_call(x, g, b):
    n, d = x.shape
    row = pl.BlockSpec((ROW_TILE, d), lambda i: (i, 0))
    return pl.pallas_call(
        _ln_kernel, grid=(n // ROW_TILE,),
        in_specs=[row, _const_spec((1, d)), _const_spec((1, d))],
        out_specs=row, out_shape=jax.ShapeDtypeStruct((n, d), F32),
        compiler_params=_params("parallel"), name="ln0")(x, g.reshape(1, d), b.reshape(1, d))


_IN_SPLITS = (D_ATTN, D_ATTN, D_ATTN, D_SSM, D_LRU, D_LRU)


def _in_proj_kernel(h_ref, w_ref, *out_refs):
    hb = h_ref[...].astype(BF16)
    start = 0
    for o_ref, width in zip(out_refs, _IN_SPLITS):
        o_ref[...] = _dot(hb, w_ref[:, start:start + width])
        start += width


def _in_proj_call(h, w_in):
    n, d = h.shape
    return pl.pallas_call(
        _in_proj_kernel, grid=(n // ROW_TILE,),
        in_specs=[pl.BlockSpec((ROW_TILE, d), lambda i: (i, 0)), _const_spec(w_in.shape)],
        out_specs=[pl.BlockSpec((ROW_TILE, w), lambda i: (i, 0)) for w in _IN_SPLITS],
        out_shape=[jax.ShapeDtypeStruct((n, w), F32) for w in _IN_SPLITS],
        compiler_params=_params("parallel"), name="in_proj")(h, w_in)


def _moba_kernel(slopes_ref, q_ref, k_ref, v_ref, o_ref,
                 kt_scr, vb_scr, kmean_scr, m_scr, l_scr, acc_scr):
    hg = pl.program_id(1)
    i = pl.program_id(2)
    nb = kt_scr.shape[0]
    blk = MOBA_BLOCK
    width = HEADS_PER_STEP * ATTN_HEAD_DIM

    @pl.when(i == 0)
    def _():
        kmean_scr[...] = jnp.zeros_like(kmean_scr)
        for j in range(nb):
            kj = k_ref[j * blk:(j + 1) * blk, :]
            kmean_scr[j:j + 1, :] = jnp.mean(kj, axis=0, keepdims=True)
            kt_scr[j] = kj.T.astype(BF16)
            vb_scr[j] = v_ref[j * blk:(j + 1) * blk, :].astype(BF16)

    q = q_ref[...]
    lane_head = lax.shift_right_logical(lax.broadcasted_iota(jnp.int32, (blk, width), 1),
                                        int(math.log2(ATTN_HEAD_DIM)))
    col = lax.broadcasted_iota(jnp.int32, (blk, LANES), 1)
    rc = (lax.broadcasted_iota(jnp.int32, (blk, blk), 0)
          - lax.broadcasted_iota(jnp.int32, (blk, blk), 1))
    causal = rc >= 0
    rc_f = rc.astype(F32)
    scale = ATTN_HEAD_DIM ** -0.5
    kmean = kmean_scr[...]
    o_ref[...] = jnp.zeros_like(o_ref)

    def head_body(hh, carry):
        mine = lane_head == hh
        qh = jnp.where(mine, q, 0.0)
        qb = qh.astype(BF16)
        slope = slopes_ref[hg * HEADS_PER_STEP + hh]

        gate = _dot3(qh, kmean, _dot_nt)
        gate = jnp.where(col < i, gate, NEG_INF)
        rank = jnp.zeros((blk, LANES), jnp.int32)
        for jp in range(nb):
            gj = gate[:, jp:jp + 1]
            beats = (gj > gate) | ((gj == gate) & (col > jp))
            rank = rank + beats.astype(jnp.int32)
        selected = (rank < MOBA_TOPK) & (col < i)

        bias = slope * rc_f
        s = _dot(qb, kt_scr[i]) * scale
        s = jnp.where(causal, s - bias, NEG_INF)
        m0 = jnp.max(s, axis=-1, keepdims=True)
        p = jnp.exp(s - m0)
        m_scr[...] = m0
        l_scr[...] = jnp.sum(p, axis=-1, keepdims=True)
        acc_scr[...] = _dot(p.astype(BF16), vb_scr[i])

        for j in range(nb - 1):
            @pl.when(j < i)
            def _():
                sel_j = selected[:, j:j + 1]
                off = slope * ((i - j) * blk).astype(F32)
                sj = _dot(qb, kt_scr[j]) * scale - bias - off
                sj = jnp.where(sel_j, sj, NEG_INF)
                m_old = m_scr[...]
                m_new = jnp.maximum(m_old, jnp.max(sj, axis=-1, keepdims=True))
                corr = jnp.exp(m_old - m_new)
                pj = jnp.exp(sj - m_new)
                m_scr[...] = m_new
                l_scr[...] = corr * l_scr[...] + jnp.sum(pj, axis=-1, keepdims=True)
                acc_scr[...] = corr * acc_scr[...] + _dot(pj.astype(BF16), vb_scr[j])

        o_ref[...] = jnp.where(mine, acc_scr[...] / l_scr[...], o_ref[...])
        return carry

    lax.fori_loop(0, HEADS_PER_STEP, head_body, 0)


def _moba_call(q, k, v, bsz, seq):
    nb = seq // MOBA_BLOCK
    width = HEADS_PER_STEP * ATTN_HEAD_DIM
    slopes = jnp.asarray([2.0 ** (-8.0 * (h + 1) / ATTN_HEADS) for h in range(ATTN_HEADS)], F32)
    qspec = pl.BlockSpec((MOBA_BLOCK, width), lambda b, g, i: (b * nb + i, g))
    kvspec = pl.BlockSpec((seq, width), lambda b, g, i: (b, g))
    return pl.pallas_call(
        _moba_kernel, grid=(bsz, D_ATTN // width, nb),
        in_specs=[pl.BlockSpec(memory_space=pltpu.SMEM), qspec, kvspec, kvspec],
        out_specs=qspec, out_shape=jax.ShapeDtypeStruct(q.shape, F32),
        scratch_shapes=[pltpu.VMEM((nb, width, MOBA_BLOCK), BF16),
                        pltpu.VMEM((nb, MOBA_BLOCK, width), BF16),
                        pltpu.VMEM((LANES, width), F32),
                        pltpu.VMEM((MOBA_BLOCK, 1), F32),
                        pltpu.VMEM((MOBA_BLOCK, 1), F32),
                        pltpu.VMEM((MOBA_BLOCK, width), F32)],
        compiler_params=_params("parallel", "parallel", "arbitrary"), name="moba")(slopes, q, k, v)


def _s5_kernel(u_ref, bmat_ref, lvl_ref, cpow_ref, cmat_ref, d_ref, wglu_ref, bglu_ref, o_ref,
               x_scr, carry_scr):
    c = pl.program_id(1)
    chunk = u_ref.shape[0]
    ns = D_STATE

    @pl.when(c == 0)
    def _():
        carry_scr[...] = jnp.zeros_like(carry_scr)

    u = u_ref[...]
    x_scr[...] = _dot(u.astype(BF16), bmat_ref[...])

    def tile_body(t, carry):
        cr, ci = carry
        rows = pl.ds(pl.multiple_of(t * SUBLANES, SUBLANES), SUBLANES)
        br = x_scr[rows, 0:ns]
        bi = x_scr[rows, ns:2 * ns]
        for lvl in range(3):
            pr = lvl_ref[lvl, :, 0:ns]
            pi = lvl_ref[lvl, :, ns:2 * ns]
            sr = pltpu.roll(br, 1 << lvl, 0)
            si = pltpu.roll(bi, 1 << lvl, 0)
            br, bi = br + pr * sr - pi * si, bi + pr * si + pi * sr
        ar = cpow_ref[:, 0:ns]
        ai = cpow_ref[:, ns:2 * ns]
        xr = br + ar * cr - ai * ci
        xi = bi + ar * ci + ai * cr
        x_scr[rows, 0:ns] = xr
        x_scr[rows, ns:2 * ns] = xi
        return xr[SUBLANES - 1:SUBLANES, :], xi[SUBLANES - 1:SUBLANES, :]

    cr, ci = lax.fori_loop(0, chunk // SUBLANES, tile_body,
                           (carry_scr[:, 0:ns], carry_scr[:, ns:2 * ns]), unroll=2)
    carry_scr[:, 0:ns] = cr
    carry_scr[:, ns:2 * ns] = ci

    y = _dot(x_scr[...].astype(BF16), cmat_ref[...])
    y = jax.nn.gelu(y + d_ref[...] * u)
    gate = jax.nn.sigmoid(_dot(y.astype(BF16), wglu_ref[...]) + bglu_ref[...])
    o_ref[...] = y * gate


def _complex_mul(ar, ai, br, bi):
    return ar * br - ai * bi, ar * bi + ai * br


def _s5_tables(a_re, a_im, b_re, b_im, c_re, c_im, log_dt):
    g, p, h = b_re.shape
    dt = jnp.exp(log_dt)[:, None]
    mag = jnp.exp(dt * a_re)
    ab_re, ab_im = mag * jnp.cos(dt * a_im), mag * jnp.sin(dt * a_im)
    den = a_re * a_re + a_im * a_im
    nr, ni = ab_re - 1.0, ab_im
    f_re = (nr * a_re + ni * a_im) / den
    f_im = (ni * a_re - nr * a_im) / den
    bb_re = f_re[..., None] * b_re - f_im[..., None] * b_im
    bb_im = f_re[..., None] * b_im + f_im[..., None] * b_re
    eye = jnp.eye(g, dtype=F32)
    bmat_re = jnp.einsum('gph,gk->ghkp', bb_re, eye).reshape(g * h, g * p)
    bmat_im = jnp.einsum('gph,gk->ghkp', bb_im, eye).reshape(g * h, g * p)
    bmat = jnp.concatenate([bmat_re, bmat_im], axis=1).astype(BF16)
    cmat_re = jnp.einsum('ghp,gk->gpkh', c_re, eye).reshape(g * p, g * h)
    cmat_im = jnp.einsum('ghp,gk->gpkh', c_im, eye).reshape(g * p, g * h)
    cmat = jnp.concatenate([cmat_re, -cmat_im], axis=0).astype(BF16)
    pows = [(ab_re.reshape(1, g * p), ab_im.reshape(1, g * p))]
    for _ in range(SUBLANES - 1):
        pows.append(_complex_mul(*pows[-1], *pows[0]))
    sub = jnp.arange(SUBLANES)[:, None]
    lvl = []
    for k in range(3):
        pr, pi = pows[(1 << k) - 1]
        keep = sub >= (1 << k)
        lvl.append(jnp.concatenate([jnp.where(keep, pr, 0.0), jnp.where(keep, pi, 0.0)], axis=1))
    lvl = jnp.stack(lvl)
    cpow = jnp.concatenate([jnp.concatenate([pw[0] for pw in pows], axis=0),
                            jnp.concatenate([pw[1] for pw in pows], axis=0)], axis=1)
    return bmat, lvl, cpow, cmat


def _s5_call(u, tables, d_skip, w_glu, b_glu, bsz, seq):
    bmat, lvl, cpow, cmat = tables
    n, d = u.shape
    nc = seq // SCAN_CHUNK
    row = pl.BlockSpec((SCAN_CHUNK, d), lambda b, c: (b * nc + c, 0))
    return pl.pallas_call(
        _s5_kernel, grid=(bsz, nc),
        in_specs=[row, _const_spec(bmat.shape), _const_spec(lvl.shape), _const_spec(cpow.shape),
                  _const_spec(cmat.shape), _const_spec((1, d)), _const_spec((d, d)),
                  _const_spec((1, d))],
        out_specs=row, out_shape=jax.ShapeDtypeStruct((n, d), F32),
        scratch_shapes=[pltpu.VMEM((SCAN_CHUNK, 2 * D_STATE), F32),
                        pltpu.VMEM((1, 2 * D_STATE), F32)],
        compiler_params=_params("parallel", "arbitrary"), name="s5")(
            u, bmat, lvl, cpow, cmat, d_skip.reshape(1, d), w_glu.astype(BF16), b_glu.reshape(1, d))


def _lru_kernel(xl_ref, gl_ref, cw_ref, cb_ref, wa_ref, ba_ref, wx_ref, bx_ref, lam_ref, o_ref,
                ext_scr, a_scr, h_scr, carry_scr):
    c = pl.program_id(1)
    chunk = xl_ref.shape[0]
    halo = SUBLANES

    @pl.when(c == 0)
    def _():
        ext_scr[0:halo, :] = jnp.zeros((halo, ext_scr.shape[1]), F32)
        carry_scr[...] = jnp.zeros_like(carry_scr)

    xl = xl_ref[...]
    ext_scr[halo:, :] = xl
    ext = ext_scr[...]
    xc = cb_ref[...] + cw_ref[CONV_WIDTH - 1:CONV_WIDTH, :] * xl
    for back in range(1, CONV_WIDTH):
        shifted = pltpu.roll(ext, back, 0)[halo:, :]
        xc = xc + cw_ref[CONV_WIDTH - 1 - back:CONV_WIDTH - back, :] * shifted
    ext_scr[0:halo, :] = xl[chunk - halo:, :]

    xb = xc.astype(BF16)
    r = jax.nn.sigmoid(_dot(xb, wa_ref[...]) + ba_ref[...])
    gate_in = jax.nn.sigmoid(_dot(xb, wx_ref[...]) + bx_ref[...])
    neg_lam = -lam_ref[...]
    softplus = jnp.maximum(neg_lam, 0.0) + jnp.log1p(jnp.exp(-jnp.abs(neg_lam)))
    log_a = -LRU_C * r * softplus
    a = jnp.exp(log_a)
    one_minus_a2 = jnp.tanh(-log_a) * (1.0 + a * a)
    a_scr[...] = a
    h_scr[...] = jnp.sqrt(one_minus_a2) * (gate_in * xc)

    sub = lax.broadcasted_iota(jnp.int32, (SUBLANES, a.shape[1]), 0)

    def tile_body(t, carry):
        rows = pl.ds(pl.multiple_of(t * SUBLANES, SUBLANES), SUBLANES)
        at = a_scr[rows, :]
        bt = h_scr[rows, :]
        for lvl in range(3):
            keep = sub >= (1 << lvl)
            a_sh = jnp.where(keep, pltpu.roll(at, 1 << lvl, 0), 1.0)
            b_sh = jnp.where(keep, pltpu.roll(bt, 1 << lvl, 0), 0.0)
            bt = bt + at * b_sh
            at = at * a_sh
        ht = bt + at * carry
        h_scr[rows, :] = ht
        return ht[SUBLANES - 1:SUBLANES, :]

    carry_scr[...] = lax.fori_loop(0, chunk // SUBLANES, tile_body, carry_scr[...], unroll=2)
    o_ref[...] = h_scr[...] * jax.nn.gelu(gl_ref[...])


def _block_diag(w):
    heads, di, do = w.shape
    eye = jnp.eye(heads, dtype=w.dtype)
    return jnp.einsum('hij,hk->hikj', w, eye).reshape(heads * di, heads * do)


def _lru_call(xl, gl, conv_w, conv_b, w_a, b_a, w_x, b_x, lam, bsz, seq):
    n, d = xl.shape
    nc = seq // SCAN_CHUNK
    row = pl.BlockSpec((SCAN_CHUNK, d), lambda b, c: (b * nc + c, 0))
    vec = _const_spec((1, d))
    mat = _const_spec((d, d))
    return pl.pallas_call(
        _lru_kernel, grid=(bsz, nc),
        in_specs=[row, row, _const_spec((CONV_WIDTH, d)), vec, mat, vec, mat, vec, vec],
        out_specs=row, out_shape=jax.ShapeDtypeStruct((n, d), F32),
        scratch_shapes=[pltpu.VMEM((SCAN_CHUNK + SUBLANES, d), F32),
                        pltpu.VMEM((SCAN_CHUNK, d), F32),
                        pltpu.VMEM((SCAN_CHUNK, d), F32),
                        pltpu.VMEM((1, d), F32)],
        compiler_params=_params("parallel", "arbitrary"), name="rglru")(
            xl, gl, conv_w, conv_b.reshape(1, d), _block_diag(w_a).astype(BF16), b_a.reshape(1, d),
            _block_diag(w_x).astype(BF16), b_x.reshape(1, d), lam.reshape(1, d))


def _out_proj_kernel(ya_ref, ys_ref, yl_ref, h_ref, mg_ref, w_ref, g_ref, b_ref, o_ref):
    mix = None
    start = 0
    for y_ref in (ya_ref, ys_ref, yl_ref):
        width = y_ref.shape[1]
        y = (_rms(y_ref[...]) * mg_ref[:, start:start + width]).astype(BF16)
        part = _dot(y, w_ref[start:start + width, :])
        mix = part if mix is None else mix + part
        start += width
    o_ref[...] = _layer_norm(ALPHA * h_ref[...] + mix, g_ref[...], b_ref[...])


def _out_proj_call(ya, ys, yl, h, mix_g, w_out, g, b):
    n, d = h.shape
    row = lambda w: pl.BlockSpec((ROW_TILE, w), lambda i: (i, 0))
    vec = _const_spec((1, d))
    return pl.pallas_call(
        _out_proj_kernel, grid=(n // ROW_TILE,),
        in_specs=[row(ya.shape[1]), row(ys.shape[1]), row(yl.shape[1]), row(d), vec,
                  _const_spec(w_out.shape), vec, vec],
        out_specs=row(d), out_shape=jax.ShapeDtypeStruct((n, d), F32),
        compiler_params=_params("parallel"), name="out_proj")(
            ya, ys, yl, h, mix_g.reshape(1, d), w_out, g.reshape(1, d), b.reshape(1, d))


def _mem_kv_kernel(mem_ref, wk_ref, wv_ref, kt_ref, v_ref):
    mb = mem_ref[...].astype(BF16)
    kt_ref[0] = _dot(mb, wk_ref[...]).T.astype(BF16)
    v_ref[0] = _dot(mb, wv_ref[...]).astype(BF16)


def _mem_kv_call(mem, wk, wv):
    bsz, m, d = mem.shape
    return pl.pallas_call(
        _mem_kv_kernel, grid=(bsz,),
        in_specs=[pl.BlockSpec((m, d), lambda b: (b, 0)), _const_spec((d, d)), _const_spec((d, d))],
        out_specs=[pl.BlockSpec((1, d, m), lambda b: (b, 0, 0)),
                   pl.BlockSpec((1, m, d), lambda b: (b, 0, 0))],
        out_shape=[jax.ShapeDtypeStruct((bsz, d, m), BF16), jax.ShapeDtypeStruct((bsz, m, d), BF16)],
        compiler_params=_params("parallel"), name="mem_kv")(mem.reshape(bsz * m, d), wk, wv)


def _cross_kernel(h_ref, kt_ref, v_ref, wq_ref, wo_ref, g_ref, b_ref, o_ref, ctx_scr):
    h = h_ref[...]
    d = h.shape[1]
    hd = d // MEM_HEADS
    q = _dot(h.astype(BF16), wq_ref[...])
    scale = hd ** -0.5
    for hh in range(MEM_HEADS):
        cols = slice(hh * hd, (hh + 1) * hd)
        s = _dot(q[:, cols].astype(BF16), kt_ref[0, cols, :]) * scale
        s = s - jnp.max(s, axis=-1, keepdims=True)
        p = jnp.exp(s)
        p = p / jnp.sum(p, axis=-1, keepdims=True)
        ctx_scr[:, cols] = _dot(p.astype(BF16), v_ref[0, :, cols]).astype(BF16)
    cross = _dot(ctx_scr[...], wo_ref[...])
    o_ref[...] = _layer_norm(ALPHA * h + cross, g_ref[...], b_ref[...])


def _cross_call(h, kt, v, wq, wo, g, b, seq):
    n, d = h.shape
    m = v.shape[1]
    tiles_per_seq = seq // ROW_TILE
    row = pl.BlockSpec((ROW_TILE, d), lambda i: (i, 0))
    vec = _const_spec((1, d))
    return pl.pallas_call(
        _cross_kernel, grid=(n // ROW_TILE,),
        in_specs=[row, pl.BlockSpec((1, d, m), lambda i: (i // tiles_per_seq, 0, 0)),
                  pl.BlockSpec((1, m, d), lambda i: (i // tiles_per_seq, 0, 0)),
                  _const_spec((d, d)), _const_spec((d, d)), vec, vec],
        out_specs=row, out_shape=jax.ShapeDtypeStruct((n, d), F32),
        scratch_shapes=[pltpu.VMEM((ROW_TILE, d), BF16)],
        compiler_params=_params("parallel"), name="cross_attn")(
            h, kt, v, wq, wo, g.reshape(1, d), b.reshape(1, d))


def _first_max(x, valid, lane):
    xm = jnp.where(valid, x, -jnp.inf)
    mx = jnp.max(xm, axis=-1, keepdims=True)
    idx = jnp.min(jnp.where(valid & (xm == mx), lane, LANES), axis=-1, keepdims=True)
    return xm, mx, idx


def _route(h, wr_ref, br_ref):
    rows = h.shape[0]
    logits = _dot3(h, wr_ref[...], _dot) + br_ref[...]
    lane = lax.broadcasted_iota(jnp.int32, (rows, LANES), 1)
    g_logits, g_max, g_idx = _first_max(logits, lane < N_GROUPS, lane)
    g_w = 1.0 / jnp.sum(jnp.exp(g_logits - g_max), axis=-1, keepdims=True)
    expert = lane - N_GROUPS
    group_of = lax.shift_right_arithmetic(expert, int(math.log2(EXPERTS_PER_GROUP)))
    in_group = (expert >= 0) & (expert < N_EXPERTS) & (group_of == g_idx)
    e_logits, e_max, _ = _first_max(logits, in_group, lane)
    e_exp = jnp.exp(e_logits - e_max)
    e_prob = e_exp / jnp.sum(e_exp, axis=-1, keepdims=True)
    _, p1, i1 = _first_max(e_prob, in_group, lane)
    _, p2, i2 = _first_max(e_prob, in_group & (lane != i1), lane)
    denom = p1 + p2
    local = jnp.where(lane == i1, p1 / denom, jnp.where(lane == i2, p2 / denom, 0.0))
    return local * g_w


def _moe_kernel(h_ref, wr_ref, br_ref, wg_ref, wu_ref, wd_ref, g_ref, b_ref, o_ref,
                hb_scr, comb_scr, acc_scr):
    e = pl.program_id(1)

    @pl.when(e == 0)
    def _():
        h = h_ref[...]
        hb_scr[...] = h.astype(BF16)
        comb_scr[...] = _route(h, wr_ref, br_ref)
        acc_scr[...] = jnp.zeros_like(acc_scr)

    hb = hb_scr[...]
    he = jax.nn.silu(_dot(hb, wg_ref[0])) * _dot(hb, wu_ref[0])
    y = _dot(he.astype(BF16), wd_ref[0])
    lane = lax.broadcasted_iota(jnp.int32, comb_scr.shape, 1)
    weight = jnp.sum(jnp.where(lane == e + N_GROUPS, comb_scr[...], 0.0), axis=-1, keepdims=True)
    acc_scr[...] += weight * y

    @pl.when(e == pl.num_programs(1) - 1)
    def _():
        o_ref[...] = _layer_norm(ALPHA * h_ref[...] + acc_scr[...], g_ref[...], b_ref[...])


def _moe_call(h, wr, br, w_gate, w_up, w_down, g, b):
    n, d = h.shape
    ne, _, de = w_gate.shape
    row = pl.BlockSpec((ROW_TILE, d), lambda i, e: (i, 0))
    vec = _const_spec((1, d))
    return pl.pallas_call(
        _moe_kernel, grid=(n // ROW_TILE, ne),
        in_specs=[row, _const_spec(wr.shape), _const_spec(br.shape),
                  pl.BlockSpec((1, d, de), lambda i, e: (e, 0, 0)),
                  pl.BlockSpec((1, d, de), lambda i, e: (e, 0, 0)),
                  pl.BlockSpec((1, de, d), lambda i, e: (e, 0, 0)), vec, vec],
        out_specs=row, out_shape=jax.ShapeDtypeStruct((n, d), F32),
        scratch_shapes=[pltpu.VMEM((ROW_TILE, d), BF16), pltpu.VMEM((ROW_TILE, LANES), F32),
                        pltpu.VMEM((ROW_TILE, d), F32)],
        compiler_params=_params("parallel", "arbitrary"), name="moe")(
            h, wr, br, w_gate, w_up, w_down, g.reshape(1, d), b.reshape(1, d))


def _moe_full(h, wr_g, br_g, wr_e, br_e, w_gate, w_up, w_down, g, b):
    d = wr_g.shape[0]
    pad = LANES - N_GROUPS - N_EXPERTS
    wr = jnp.concatenate([wr_g, wr_e, jnp.zeros((d, pad), F32)], axis=1)
    br = jnp.concatenate([br_g, br_e, jnp.zeros((pad,), F32)]).reshape(1, LANES)
    return _moe_call(h, wr, br, w_gate.astype(BF16), w_up.astype(BF16), w_down.astype(BF16), g, b)


def kernel(x, mem, ln0_g, ln0_b, w_in, mix_g, w_out, ssm_a_re, ssm_a_im, ssm_b_re, ssm_b_im, ssm_c_re, ssm_c_im, ssm_d, ssm_log_dt, ssm_w_glu, ssm_b_glu, lru_conv_w, lru_conv_b, lru_w_a, lru_b_a, lru_w_x, lru_b_x, lru_lam, ln1_g, ln1_b, mem_wq, mem_wk, mem_wv, mem_wo, ln2_g, ln2_b, moe_wr_g, moe_br_g, moe_wr_e, moe_br_e, moe_w_gate, moe_w_up, moe_w_down, ln3_g, ln3_b):
    bsz, seq, d = x.shape
    depth = w_in.shape[0]
    assert seq % MOBA_BLOCK == 0 and seq % SCAN_CHUNK == 0 and seq % ROW_TILE == 0
    assert seq // MOBA_BLOCK <= SUBLANES
    h = _ln_call(x.reshape(bsz * seq, d), ln0_g, ln0_b)
    for l in range(depth):
        q, k, v, u, xl, gl = _in_proj_call(h, w_in[l].astype(BF16))
        y_attn = _moba_call(q, k, v, bsz, seq)
        tables = _s5_tables(ssm_a_re[l], ssm_a_im[l], ssm_b_re[l], ssm_b_im[l], ssm_c_re[l],
                            ssm_c_im[l], ssm_log_dt[l])
        y_ssm = _s5_call(u, tables, ssm_d[l], ssm_w_glu[l], ssm_b_glu[l], bsz, seq)
        y_lru = _lru_call(xl, gl, lru_conv_w[l], lru_conv_b[l], lru_w_a[l], lru_b_a[l],
                          lru_w_x[l], lru_b_x[l], lru_lam[l], bsz, seq)
        h = _out_proj_call(y_attn, y_ssm, y_lru, h, mix_g[l], w_out[l].astype(BF16),
                           ln1_g[l], ln1_b[l])
        kt, vm = _mem_kv_call(mem, mem_wk[l].astype(BF16), mem_wv[l].astype(BF16))
        h = _cross_call(h, kt, vm, mem_wq[l].astype(BF16), mem_wo[l].astype(BF16),
                        ln2_g[l], ln2_b[l], seq)
        h = _moe_full(h, moe_wr_g[l], moe_br_g[l], moe_wr_e[l], moe_br_e[l], moe_w_gate[l],
                      moe_w_up[l], moe_w_down[l], ln3_g[l], ln3_b[l])
    return h.reshape(bsz, seq, d)
```

```python
import functools
import math

import jax
import jax.numpy as jnp
from jax import lax
from jax.experimental import pallas as pl
from jax.experimental.pallas import tpu as pltpu

F32 = jnp.float32
BF16 = jnp.bfloat16

D_ATTN = 512
D_SSM = 256
D_LRU = 256
SSM_GROUPS = 16
SSM_GROUP = 16
SSM_STATE = 64
D_STATE = SSM_GROUPS * SSM_STATE
LRU_HEADS = 4
CONV_WIDTH = 4
LRU_C = 8.0
ATTN_HEADS = 8
ATTN_HEAD_DIM = 64
HEADS_PER_STEP = 4
MOBA_BLOCK = 256
MOBA_TOPK = 3
MEM_HEADS = 4
N_GROUPS = 4
EXPERTS_PER_GROUP = 4
N_EXPERTS = 16
DEPTH = 2
ALPHA = (2.0 * DEPTH) ** 0.25
LN_EPS = 1e-5
RMS_EPS = 1e-6
NEG_INF = -1e30

SUBLANES = 8
LANES = 128
VMEM_LIMIT = 48 * 1024 * 1024
ROW_TILE = 512
SCAN_CHUNK = 512


def _params(*sem):
    return pltpu.CompilerParams(dimension_semantics=sem, vmem_limit_bytes=VMEM_LIMIT)


def _const_spec(shape):
    zeros = (0,) * len(shape)
    return pl.BlockSpec(shape, lambda *_: zeros)


def _layer_norm(x, g, b):
    mu = jnp.mean(x, axis=-1, keepdims=True)
    xc = x - mu
    var = jnp.mean(xc * xc, axis=-1, keepdims=True)
    return xc * lax.rsqrt(var + LN_EPS) * g + b


def _rms(x):
    return x * lax.rsqrt(jnp.mean(x * x, axis=-1, keepdims=True) + RMS_EPS)


def _dot(a, b):
    return jnp.dot(a, b, preferred_element_type=F32)


def _dot_nt(a, b):
    return lax.dot_general(a, b, (((1,), (1,)), ((), ())), preferred_element_type=F32)


def _split_bf16(x):
    hi = x.astype(BF16)
    lo = (x - hi.astype(F32)).astype(BF16)
    return hi, lo


def _dot3(a, b, dot):
    ah, al = _split_bf16(a)
    bh, bl = _split_bf16(b)
    return dot(ah, bh) + dot(ah, bl) + dot(al, bh)


def _ln_kernel(x_ref, g_ref, b_ref, o_ref):
    o_ref[...] = _layer_norm(x_ref[...], g_ref[...], b_ref[...])


def _ln_call(x, g, b):
    n, d = x.shape
    row = pl.BlockSpec((ROW_TILE, d), lambda i: (i, 0))
    return pl.pallas_call(
        _ln_kernel, grid=(n // ROW_TILE,),
        in_specs=[row, _const_spec((1, d)), _const_spec((1, d))],
        out_specs=row, out_shape=jax.ShapeDtypeStruct((n, d), F32),
        compiler_params=_params("parallel"), name="ln0")(x, g.reshape(1, d), b.reshape(1, d))


_IN_SPLITS = (D_ATTN, D_ATTN, D_ATTN, D_SSM, D_LRU, D_LRU)


def _in_proj_kernel(h_ref, w_ref, *out_refs):
    hb = h_ref[...].astype(BF16)
    start = 0
    for o_ref, width in zip(out_refs, _IN_SPLITS):
        o_ref[...] = _dot(hb, w_ref[:, start:start + width])
        start += width


def _in_proj_call(h, w_in):
    n, d = h.shape
    return pl.pallas_call(
        _in_proj_kernel, grid=(n // ROW_TILE,),
        in_specs=[pl.BlockSpec((ROW_TILE, d), lambda i: (i, 0)), _const_spec(w_in.shape)],
        out_specs=[pl.BlockSpec((ROW_TILE, w), lambda i: (i, 0)) for w in _IN_SPLITS],
        out_shape=[jax.ShapeDtypeStruct((n, w), F32) for w in _IN_SPLITS],
        compiler_params=_params("parallel"), name="in_proj")(h, w_in)


AUG_ROWS = 16
GATE_LANE0 = 2
GATE_STRIDE = 32


def _moba_prepare(k_ref, v_ref, kt_scr, vb_scr, kmh_scr, kml_scr, nb):
    blk = MOBA_BLOCK
    width = k_ref.shape[1]
    arow = lax.broadcasted_iota(jnp.int32, (AUG_ROWS, blk), 0)
    akey = lax.broadcasted_iota(jnp.int32, (AUG_ROWS, blk), 1).astype(F32)
    lane_head = lax.shift_right_logical(lax.broadcasted_iota(jnp.int32, (1, width), 1),
                                        int(math.log2(ATTN_HEAD_DIM)))
    gate_rows = [jnp.zeros((GATE_LANE0, width), F32)]
    for j in range(nb):
        kj = k_ref[j * blk:(j + 1) * blk, :]
        ktj = kj.T.astype(BF16)
        aug = jnp.where(arow == 0, float(blk * j),
                        jnp.where(arow == 1, akey, jnp.where(arow == GATE_LANE0 + j, 1.0, 0.0)))
        aug = aug.astype(BF16)
        kt_scr[0, j] = ktj
        kt_scr[0, j, LANES:LANES + AUG_ROWS, :] = aug
        kt_scr[1, j] = ktj
        kt_scr[1, j, 0:AUG_ROWS, :] = aug
        vb_scr[j * blk:(j + 1) * blk, :] = v_ref[j * blk:(j + 1) * blk, :].astype(BF16)
        gate_rows.append(jnp.mean(kj, axis=0, keepdims=True))
    gate_rows.append(jnp.zeros((GATE_STRIDE - GATE_LANE0 - nb, width), F32))
    kmean = jnp.concatenate(gate_rows, axis=0)
    table = jnp.concatenate([jnp.where(lane_head == h, kmean, 0.0) for h in range(HEADS_PER_STEP)], axis=0)
    hi, lo = _split_bf16(table)
    kmh_scr[...] = hi
    kml_scr[...] = lo


def _moba_block(c, hg, slopes_ref, q_ref, o_ref, kt_scr, vb_scr, kmh_scr, kml_scr, nb):
    blk = MOBA_BLOCK
    width = q_ref.shape[1]
    q = q_ref[...]
    lane_head = lax.shift_right_logical(lax.broadcasted_iota(jnp.int32, (blk, width), 1),
                                        int(math.log2(ATTN_HEAD_DIM)))
    col = lax.broadcasted_iota(jnp.int32, (blk, LANES), 1)
    blockid = col - GATE_LANE0
    valid = (blockid >= 0) & (blockid < c)
    causal = (lax.broadcasted_iota(jnp.int32, (blk, blk), 0)
              >= lax.broadcasted_iota(jnp.int32, (blk, blk), 1))
    zeros = jnp.zeros((blk, LANES), F32)
    if c > MOBA_TOPK:
        qhi, qlo = _split_bf16(q)
        gate_all = (_dot_nt(qhi, kmh_scr[...]) + _dot_nt(qhi, kml_scr[...])
                    + _dot_nt(qlo, kmh_scr[...]))
    out = jnp.zeros((blk, width), F32)
    for hh in range(HEADS_PER_STEP):
        mine = lane_head == hh
        slope = slopes_ref[hg * HEADS_PER_STEP + hh]
        if c > MOBA_TOPK:
            gate = gate_all if hh == 0 else pltpu.roll(gate_all, LANES - GATE_STRIDE * hh, 1)
            gate = jnp.where(valid, gate, NEG_INF)
            rank = jnp.zeros((blk, LANES), jnp.int32)
            for jp in range(c):
                gj = gate[:, GATE_LANE0 + jp:GATE_LANE0 + jp + 1]
                beats = (gj > gate) | ((gj == gate) & (blockid > jp))
                rank = rank + beats.astype(jnp.int32)
            dropped = valid & (rank >= MOBA_TOPK)
        else:
            dropped = None
        extra = jnp.where(col < GATE_LANE0, slope, 0.0)
        if dropped is not None:
            extra = jnp.where(dropped, NEG_INF, extra)
        extra = jnp.concatenate([zeros, extra] if hh < HEADS_PER_STEP // 2 else [extra, zeros], axis=1)
        qa = jnp.where(mine, q * ATTN_HEAD_DIM ** -0.5, extra).astype(BF16)
        var = 0 if hh < HEADS_PER_STEP // 2 else 1
        tiles = [_dot(qa, kt_scr[var, j]) for j in range(c)]
        tiles.append(jnp.where(causal, _dot(qa, kt_scr[var, c]), NEG_INF))
        mx = tiles[0]
        for t in tiles[1:]:
            mx = jnp.maximum(mx, t)
        m = jnp.max(mx, axis=-1, keepdims=True)
        probs = [jnp.exp(t - m) for t in tiles]
        tot = probs[0]
        for p in probs[1:]:
            tot = tot + p
        l = jnp.sum(tot, axis=-1, keepdims=True)
        pb = jnp.concatenate([p.astype(BF16) for p in probs], axis=1)
        ctx = _dot(pb, vb_scr[0:(c + 1) * blk, :])
        out = jnp.where(mine, ctx / l, out)
    o_ref[...] = out


def _moba_kernel(slopes_ref, q_ref, k_ref, v_ref, o_ref, kt_scr, vb_scr, kmh_scr, kml_scr):
    hg = pl.program_id(1)
    i = pl.program_id(2)
    nb = vb_scr.shape[0] // MOBA_BLOCK

    @pl.when(i == 0)
    def _():
        _moba_prepare(k_ref, v_ref, kt_scr, vb_scr, kmh_scr, kml_scr, nb)

    for c in range(nb):
        @pl.when(i == c)
        def _():
            _moba_block(c, hg, slopes_ref, q_ref, o_ref, kt_scr, vb_scr, kmh_scr, kml_scr, nb)


def _moba_call(q, k, v, bsz, seq):
    nb = seq // MOBA_BLOCK
    width = HEADS_PER_STEP * ATTN_HEAD_DIM
    slopes = jnp.asarray([2.0 ** (-8.0 * (h + 1) / ATTN_HEADS) for h in range(ATTN_HEADS)], F32)
    qspec = pl.BlockSpec((MOBA_BLOCK, width), lambda b, g, i: (b * nb + i, g))
    kvspec = pl.BlockSpec((seq, width), lambda b, g, i: (b, g))
    return pl.pallas_call(
        _moba_kernel, grid=(bsz, D_ATTN // width, nb),
        in_specs=[pl.BlockSpec(memory_space=pltpu.SMEM), qspec, kvspec, kvspec],
        out_specs=qspec, out_shape=jax.ShapeDtypeStruct(q.shape, F32),
        scratch_shapes=[pltpu.VMEM((2, nb, width, MOBA_BLOCK), BF16),
                        pltpu.VMEM((seq, width), BF16),
                        pltpu.VMEM((LANES, width), BF16),
                        pltpu.VMEM((LANES, width), BF16)],
        compiler_params=_params("parallel", "parallel", "arbitrary"), name="moba")(slopes, q, k, v)


def _s5_kernel(u_ref, bmat_ref, lvl_ref, cpow_ref, cmat_ref, d_ref, wglu_ref, bglu_ref, o_ref,
               x_scr, carry_scr):
    c = pl.program_id(1)
    chunk = u_ref.shape[0]
    ns = D_STATE

    @pl.when(c == 0)
    def _():
        carry_scr[...] = jnp.zeros_like(carry_scr)

    u = u_ref[...]
    x_scr[...] = _dot(u.astype(BF16), bmat_ref[...])

    def tile_body(t, carry):
        cr, ci = carry
        rows = pl.ds(pl.multiple_of(t * SUBLANES, SUBLANES), SUBLANES)
        br = x_scr[rows, 0:ns]
        bi = x_scr[rows, ns:2 * ns]
        for lvl in range(3):
            pr = lvl_ref[lvl, :, 0:ns]
            pi = lvl_ref[lvl, :, ns:2 * ns]
            sr = pltpu.roll(br, 1 << lvl, 0)
            si = pltpu.roll(bi, 1 << lvl, 0)
            br, bi = br + pr * sr - pi * si, bi + pr * si + pi * sr
        ar = cpow_ref[:, 0:ns]
        ai = cpow_ref[:, ns:2 * ns]
        xr = br + ar * cr - ai * ci
        xi = bi + ar * ci + ai * cr
        x_scr[rows, 0:ns] = xr
        x_scr[rows, ns:2 * ns] = xi
        return xr[SUBLANES - 1:SUBLANES, :], xi[SUBLANES - 1:SUBLANES, :]

    cr, ci = lax.fori_loop(0, chunk // SUBLANES, tile_body,
                           (carry_scr[:, 0:ns], carry_scr[:, ns:2 * ns]), unroll=2)
    carry_scr[:, 0:ns] = cr
    carry_scr[:, ns:2 * ns] = ci

    y = _dot(x_scr[...].astype(BF16), cmat_ref[...])
    y = jax.nn.gelu(y + d_ref[...] * u)
    gate = jax.nn.sigmoid(_dot(y.astype(BF16), wglu_ref[...]) + bglu_ref[...])
    o_ref[...] = y * gate


def _complex_mul(ar, ai, br, bi):
    return ar * br - ai * bi, ar * bi + ai * br


def _s5_tables(a_re, a_im, b_re, b_im, c_re, c_im, log_dt):
    g, p, h = b_re.shape
    dt = jnp.exp(log_dt)[:, None]
    mag = jnp.exp(dt * a_re)
    ab_re, ab_im = mag * jnp.cos(dt * a_im), mag * jnp.sin(dt * a_im)
    den = a_re * a_re + a_im * a_im
    nr, ni = ab_re - 1.0, ab_im
    f_re = (nr * a_re + ni * a_im) / den
    f_im = (ni * a_re - nr * a_im) / den
    bb_re = f_re[..., None] * b_re - f_im[..., None] * b_im
    bb_im = f_re[..., None] * b_im + f_im[..., None] * b_re
    eye = jnp.eye(g, dtype=F32)
    bmat_re = jnp.einsum('gph,gk->ghkp', bb_re, eye).reshape(g * h, g * p)
    bmat_im = jnp.einsum('gph,gk->ghkp', bb_im, eye).reshape(g * h, g * p)
    bmat = jnp.concatenate([bmat_re, bmat_im], axis=1).astype(BF16)
    cmat_re = jnp.einsum('ghp,gk->gpkh', c_re, eye).reshape(g * p, g * h)
    cmat_im = jnp.einsum('ghp,gk->gpkh', c_im, eye).reshape(g * p, g * h)
    cmat = jnp.concatenate([cmat_re, -cmat_im], axis=0).astype(BF16)
    pows = [(ab_re.reshape(1, g * p), ab_im.reshape(1, g * p))]
    for _ in range(SUBLANES - 1):
        pows.append(_complex_mul(*pows[-1], *pows[0]))
    sub = jnp.arange(SUBLANES)[:, None]
    lvl = []
    for k in range(3):
        pr, pi = pows[(1 << k) - 1]
        keep = sub >= (1 << k)
        lvl.append(jnp.concatenate([jnp.where(keep, pr, 0.0), jnp.where(keep, pi, 0.0)], axis=1))
    lvl = jnp.stack(lvl)
    cpow = jnp.concatenate([jnp.concatenate([pw[0] for pw in pows], axis=0),
                            jnp.concatenate([pw[1] for pw in pows], axis=0)], axis=1)
    return bmat, lvl, cpow, cmat


def _s5_call(u, tables, d_skip, w_glu, b_glu, bsz, seq):
    bmat, lvl, cpow, cmat = tables
    n, d = u.shape
    nc = seq // SCAN_CHUNK
    row = pl.BlockSpec((SCAN_CHUNK, d), lambda b, c: (b * nc + c, 0))
    return pl.pallas_call(
        _s5_kernel, grid=(bsz, nc),
        in_specs=[row, _const_spec(bmat.shape), _const_spec(lvl.shape), _const_spec(cpow.shape),
                  _const_spec(cmat.shape), _const_spec((1, d)), _const_spec((d, d)),
                  _const_spec((1, d))],
        out_specs=row, out_shape=jax.ShapeDtypeStruct((n, d), F32),
        scratch_shapes=[pltpu.VMEM((SCAN_CHUNK, 2 * D_STATE), F32),
                        pltpu.VMEM((1, 2 * D_STATE), F32)],
        compiler_params=_params("parallel", "arbitrary"), name="s5")(
            u, bmat, lvl, cpow, cmat, d_skip.reshape(1, d), w_glu.astype(BF16), b_glu.reshape(1, d))


def _lru_kernel(xl_ref, gl_ref, cw_ref, cb_ref, wa_ref, ba_ref, wx_ref, bx_ref, lam_ref, o_ref,
                ext_scr, a_scr, h_scr, carry_scr):
    c = pl.program_id(1)
    chunk = xl_ref.shape[0]
    halo = SUBLANES

    @pl.when(c == 0)
    def _():
        ext_scr[0:halo, :] = jnp.zeros((halo, ext_scr.shape[1]), F32)
        carry_scr[...] = jnp.zeros_like(carry_scr)

    xl = xl_ref[...]
    ext_scr[halo:, :] = xl
    ext = ext_scr[...]
    xc = cb_ref[...] + cw_ref[CONV_WIDTH - 1:CONV_WIDTH, :] * xl
    for back in range(1, CONV_WIDTH):
        shifted = pltpu.roll(ext, back, 0)[halo:, :]
        xc = xc + cw_ref[CONV_WIDTH - 1 - back:CONV_WIDTH - back, :] * shifted
    ext_scr[0:halo, :] = xl[chunk - halo:, :]

    xb = xc.astype(BF16)
    r = jax.nn.sigmoid(_dot(xb, wa_ref[...]) + ba_ref[...])
    gate_in = jax.nn.sigmoid(_dot(xb, wx_ref[...]) + bx_ref[...])
    neg_lam = -lam_ref[...]
    softplus = jnp.maximum(neg_lam, 0.0) + jnp.log1p(jnp.exp(-jnp.abs(neg_lam)))
    log_a = -LRU_C * r * softplus
    a = jnp.exp(log_a)
    one_minus_a2 = jnp.tanh(-log_a) * (1.0 + a * a)
    a_scr[...] = a
    h_scr[...] = jnp.sqrt(one_minus_a2) * (gate_in * xc)

    sub = lax.broadcasted_iota(jnp.int32, (SUBLANES, a.shape[1]), 0)

    def tile_body(t, carry):
        rows = pl.ds(pl.multiple_of(t * SUBLANES, SUBLANES), SUBLANES)
        at = a_scr[rows, :]
        bt = h_scr[rows, :]
        for lvl in range(3):
            keep = sub >= (1 << lvl)
            a_sh = jnp.where(keep, pltpu.roll(at, 1 << lvl, 0), 1.0)
            b_sh = jnp.where(keep, pltpu.roll(bt, 1 << lvl, 0), 0.0)
            bt = bt + at * b_sh
            at = at * a_sh
        ht = bt + at * carry
        h_scr[rows, :] = ht
        return ht[SUBLANES - 1:SUBLANES, :]

    carry_scr[...] = lax.fori_loop(0, chunk // SUBLANES, tile_body, carry_scr[...], unroll=2)
    o_ref[...] = h_scr[...] * jax.nn.gelu(gl_ref[...])


def _block_diag(w):
    heads, di, do = w.shape
    eye = jnp.eye(heads, dtype=w.dtype)
    return jnp.einsum('hij,hk->hikj', w, eye).reshape(heads * di, heads * do)


def _lru_call(xl, gl, conv_w, conv_b, w_a, b_a, w_x, b_x, lam, bsz, seq):
    n, d = xl.shape
    nc = seq // SCAN_CHUNK
    row = pl.BlockSpec((SCAN_CHUNK, d), lambda b, c: (b * nc + c, 0))
    vec = _const_spec((1, d))
    mat = _const_spec((d, d))
    return pl.pallas_call(
        _lru_kernel, grid=(bsz, nc),
        in_specs=[row, row, _const_spec((CONV_WIDTH, d)), vec, mat, vec, mat, vec, vec],
        out_specs=row, out_shape=jax.ShapeDtypeStruct((n, d), F32),
        scratch_shapes=[pltpu.VMEM((SCAN_CHUNK + SUBLANES, d), F32),
                        pltpu.VMEM((SCAN_CHUNK, d), F32),
                        pltpu.VMEM((SCAN_CHUNK, d), F32),
                        pltpu.VMEM((1, d), F32)],
        compiler_params=_params("parallel", "arbitrary"), name="rglru")(
            xl, gl, conv_w, conv_b.reshape(1, d), _block_diag(w_a).astype(BF16), b_a.reshape(1, d),
            _block_diag(w_x).astype(BF16), b_x.reshape(1, d), lam.reshape(1, d))


def _out_proj_kernel(ya_ref, ys_ref, yl_ref, h_ref, mg_ref, w_ref, g_ref, b_ref, o_ref):
    mix = None
    start = 0
    for y_ref in (ya_ref, ys_ref, yl_ref):
        width = y_ref.shape[1]
        y = (_rms(y_ref[...]) * mg_ref[:, start:start + width]).astype(BF16)
        part = _dot(y, w_ref[start:start + width, :])
        mix = part if mix is None else mix + part
        start += width
    o_ref[...] = _layer_norm(ALPHA * h_ref[...] + mix, g_ref[...], b_ref[...])


def _out_proj_call(ya, ys, yl, h, mix_g, w_out, g, b):
    n, d = h.shape
    row = lambda w: pl.BlockSpec((ROW_TILE, w), lambda i: (i, 0))
    vec = _const_spec((1, d))
    return pl.pallas_call(
        _out_proj_kernel, grid=(n // ROW_TILE,),
        in_specs=[row(ya.shape[1]), row(ys.shape[1]), row(yl.shape[1]), row(d), vec,
                  _const_spec(w_out.shape), vec, vec],
        out_specs=row(d), out_shape=jax.ShapeDtypeStruct((n, d), F32),
        compiler_params=_params("parallel"), name="out_proj")(
            ya, ys, yl, h, mix_g.reshape(1, d), w_out, g.reshape(1, d), b.reshape(1, d))


def _mem_kv_kernel(mem_ref, wk_ref, wv_ref, kt_ref, v_ref):
    mb = mem_ref[...].astype(BF16)
    kt_ref[0] = _dot(mb, wk_ref[...]).T.astype(BF16)
    v_ref[0] = _dot(mb, wv_ref[...]).astype(BF16)


def _mem_kv_call(mem, wk, wv):
    bsz, m, d = mem.shape
    return pl.pallas_call(
        _mem_kv_kernel, grid=(bsz,),
        in_specs=[pl.BlockSpec((m, d), lambda b: (b, 0)), _const_spec((d, d)), _const_spec((d, d))],
        out_specs=[pl.BlockSpec((1, d, m), lambda b: (b, 0, 0)),
                   pl.BlockSpec((1, m, d), lambda b: (b, 0, 0))],
        out_shape=[jax.ShapeDtypeStruct((bsz, d, m), BF16), jax.ShapeDtypeStruct((bsz, m, d), BF16)],
        compiler_params=_params("parallel"), name="mem_kv")(mem.reshape(bsz * m, d), wk, wv)


def _cross_kernel(h_ref, kt_ref, v_ref, wq_ref, wo_ref, g_ref, b_ref, o_ref, ctx_scr):
    h = h_ref[...]
    d = h.shape[1]
    hd = d // MEM_HEADS
    q = _dot(h.astype(BF16), wq_ref[...])
    scale = hd ** -0.5
    for hh in range(MEM_HEADS):
        cols = slice(hh * hd, (hh + 1) * hd)
        s = _dot(q[:, cols].astype(BF16), kt_ref[0, cols, :]) * scale
        s = s - jnp.max(s, axis=-1, keepdims=True)
        p = jnp.exp(s)
        p = p / jnp.sum(p, axis=-1, keepdims=True)
        ctx_scr[:, cols] = _dot(p.astype(BF16), v_ref[0, :, cols]).astype(BF16)
    cross = _dot(ctx_scr[...], wo_ref[...])
    o_ref[...] = _layer_norm(ALPHA * h + cross, g_ref[...], b_ref[...])


def _cross_call(h, kt, v, wq, wo, g, b, seq):
    n, d = h.shape
    m = v.shape[1]
    tiles_per_seq = seq // ROW_TILE
    row = pl.BlockSpec((ROW_TILE, d), lambda i: (i, 0))
    vec = _const_spec((1, d))
    return pl.pallas_call(
        _cross_kernel, grid=(n // ROW_TILE,),
        in_specs=[row, pl.BlockSpec((1, d, m), lambda i: (i // tiles_per_seq, 0, 0)),
                  pl.BlockSpec((1, m, d), lambda i: (i // tiles_per_seq, 0, 0)),
                  _const_spec((d, d)), _const_spec((d, d)), vec, vec],
        out_specs=row, out_shape=jax.ShapeDtypeStruct((n, d), F32),
        scratch_shapes=[pltpu.VMEM((ROW_TILE, d), BF16)],
        compiler_params=_params("parallel"), name="cross_attn")(
            h, kt, v, wq, wo, g.reshape(1, d), b.reshape(1, d))


def _first_max(x, valid, lane):
    xm = jnp.where(valid, x, -jnp.inf)
    mx = jnp.max(xm, axis=-1, keepdims=True)
    idx = jnp.min(jnp.where(valid & (xm == mx), lane, LANES), axis=-1, keepdims=True)
    return xm, mx, idx


def _route(h, wr_ref, br_ref):
    rows = h.shape[0]
    logits = _dot3(h, wr_ref[...], _dot) + br_ref[...]
    lane = lax.broadcasted_iota(jnp.int32, (rows, LANES), 1)
    g_logits, g_max, g_idx = _first_max(logits, lane < N_GROUPS, lane)
    g_w = 1.0 / jnp.sum(jnp.exp(g_logits - g_max), axis=-1, keepdims=True)
    expert = lane - N_GROUPS
    group_of = lax.shift_right_arithmetic(expert, int(math.log2(EXPERTS_PER_GROUP)))
    in_group = (expert >= 0) & (expert < N_EXPERTS) & (group_of == g_idx)
    e_logits, e_max, _ = _first_max(logits, in_group, lane)
    e_exp = jnp.exp(e_logits - e_max)
    e_prob = e_exp / jnp.sum(e_exp, axis=-1, keepdims=True)
    _, p1, i1 = _first_max(e_prob, in_group, lane)
    _, p2, i2 = _first_max(e_prob, in_group & (lane != i1), lane)
    denom = p1 + p2
    local = jnp.where(lane == i1, p1 / denom, jnp.where(lane == i2, p2 / denom, 0.0))
    return local * g_w


def _moe_kernel(h_ref, wr_ref, br_ref, wg_ref, wu_ref, wd_ref, g_ref, b_ref, o_ref,
                hb_scr, comb_scr, acc_scr):
    e = pl.program_id(1)

    @pl.when(e == 0)
    def _():
        h = h_ref[...]
        hb_scr[...] = h.astype(BF16)
        comb_scr[...] = _route(h, wr_ref, br_ref)
        acc_scr[...] = jnp.zeros_like(acc_scr)

    hb = hb_scr[...]
    he = jax.nn.silu(_dot(hb, wg_ref[0])) * _dot(hb, wu_ref[0])
    y = _dot(he.astype(BF16), wd_ref[0])
    lane = lax.broadcasted_iota(jnp.int32, comb_scr.shape, 1)
    weight = jnp.sum(jnp.where(lane == e + N_GROUPS, comb_scr[...], 0.0), axis=-1, keepdims=True)
    acc_scr[...] += weight * y

    @pl.when(e == pl.num_programs(1) - 1)
    def _():
        o_ref[...] = _layer_norm(ALPHA * h_ref[...] + acc_scr[...], g_ref[...], b_ref[...])


def _moe_call(h, wr, br, w_gate, w_up, w_down, g, b):
    n, d = h.shape
    ne, _, de = w_gate.shape
    row = pl.BlockSpec((ROW_TILE, d), lambda i, e: (i, 0))
    vec = _const_spec((1, d))
    return pl.pallas_call(
        _moe_kernel, grid=(n // ROW_TILE, ne),
        in_specs=[row, _const_spec(wr.shape), _const_spec(br.shape),
                  pl.BlockSpec((1, d, de), lambda i, e: (e, 0, 0)),
                  pl.BlockSpec((1, d, de), lambda i, e: (e, 0, 0)),
                  pl.BlockSpec((1, de, d), lambda i, e: (e, 0, 0)), vec, vec],
        out_specs=row, out_shape=jax.ShapeDtypeStruct((n, d), F32),
        scratch_shapes=[pltpu.VMEM((ROW_TILE, d), BF16), pltpu.VMEM((ROW_TILE, LANES), F32),
                        pltpu.VMEM((ROW_TILE, d), F32)],
        compiler_params=_params("parallel", "arbitrary"), name="moe")(
            h, wr, br, w_gate, w_up, w_down, g.reshape(1, d), b.reshape(1, d))


def _moe_full(h, wr_g, br_g, wr_e, br_e, w_gate, w_up, w_down, g, b):
    d = wr_g.shape[0]
    pad = LANES - N_GROUPS - N_EXPERTS
    wr = jnp.concatenate([wr_g, wr_e, jnp.zeros((d, pad), F32)], axis=1)
    br = jnp.concatenate([br_g, br_e, jnp.zeros((pad,), F32)]).reshape(1, LANES)
    return _moe_call(h, wr, br, w_gate.astype(BF16), w_up.astype(BF16), w_down.astype(BF16), g, b)


def kernel(x, mem, ln0_g, ln0_b, w_in, mix_g, w_out, ssm_a_re, ssm_a_im, ssm_b_re, ssm_b_im, ssm_c_re, ssm_c_im, ssm_d, ssm_log_dt, ssm_w_glu, ssm_b_glu, lru_conv_w, lru_conv_b, lru_w_a, lru_b_a, lru_w_x, lru_b_x, lru_lam, ln1_g, ln1_b, mem_wq, mem_wk, mem_wv, mem_wo, ln2_g, ln2_b, moe_wr_g, moe_br_g, moe_wr_e, moe_br_e, moe_w_gate, moe_w_up, moe_w_down, ln3_g, ln3_b):
    bsz, seq, d = x.shape
    depth = w_in.shape[0]
    assert seq % MOBA_BLOCK == 0 and seq % SCAN_CHUNK == 0 and seq % ROW_TILE == 0
    assert seq // MOBA_BLOCK <= SUBLANES
    h = _ln_call(x.reshape(bsz * seq, d), ln0_g, ln0_b)
    for l in range(depth):
        q, k, v, u, xl, gl = _in_proj_call(h, w_in[l].astype(BF16))
        y_attn = _moba_call(q, k, v, bsz, seq)
        tables = _s5_tables(ssm_a_re[l], ssm_a_im[l], ssm_b_re[l], ssm_b_im[l], ssm_c_re[l],
                            ssm_c_im[l], ssm_log_dt[l])
        y_ssm = _s5_call(u, tables, ssm_d[l], ssm_w_glu[l], ssm_b_glu[l], bsz, seq)
        y_lru = _lru_call(xl, gl, lru_conv_w[l], lru_conv_b[l], lru_w_a[l], lru_b_a[l],
                          lru_w_x[l], lru_b_x[l], lru_lam[l], bsz, seq)
        h = _out_proj_call(y_attn, y_ssm, y_lru, h, mix_g[l], w_out[l].astype(BF16),
                           ln1_g[l], ln1_b[l])
        kt, vm = _mem_kv_call(mem, mem_wk[l].astype(BF16), mem_wv[l].astype(BF16))
        h = _cross_call(h, kt, vm, mem_wq[l].astype(BF16), mem_wo[l].astype(BF16),
                        ln2_g[l], ln2_b[l], seq)
        h = _moe_full(h, moe_wr_g[l], moe_br_g[l], moe_wr_e[l], moe_br_e[l], moe_w_gate[l],
                      moe_w_up[l], moe_w_down[l], ln3_g[l], ln3_b[l])
    return h.reshape(bsz, seq, d)
```

```python
import functools
import math

import jax
import jax.numpy as jnp
from jax import lax
from jax.experimental import pallas as pl
from jax.experimental.pallas import tpu as pltpu

F32 = jnp.float32
BF16 = jnp.bfloat16

D_ATTN = 512
D_SSM = 256
D_LRU = 256
SSM_GROUPS = 16
SSM_GROUP = 16
SSM_STATE = 64
D_STATE = SSM_GROUPS * SSM_STATE
LRU_HEADS = 4
CONV_WIDTH = 4
LRU_C = 8.0
ATTN_HEADS = 8
ATTN_HEAD_DIM = 64
HEADS_PER_STEP = 4
MOBA_BLOCK = 256
MOBA_TOPK = 3
MEM_HEADS = 4
N_GROUPS = 4
EXPERTS_PER_GROUP = 4
N_EXPERTS = 16
DEPTH = 2
ALPHA = (2.0 * DEPTH) ** 0.25
LN_EPS = 1e-5
RMS_EPS = 1e-6
NEG_INF = -1e30

SUBLANES = 8
LANES = 128
VMEM_LIMIT = 48 * 1024 * 1024
ROW_TILE = 512
PERM_ROWS = 512
ROUTE_GROUP_LANE = 0
ROUTE_RANK_LANE = 1
PAIR_VALID, PAIR_FIRST, PAIR_LAST = 1, 2, 4
SCAN_CHUNK = 512


def _params(*sem):
    return pltpu.CompilerParams(dimension_semantics=sem, vmem_limit_bytes=VMEM_LIMIT)


def _const_spec(shape):
    zeros = (0,) * len(shape)
    return pl.BlockSpec(shape, lambda *_: zeros)


def _layer_norm(x, g, b):
    mu = jnp.mean(x, axis=-1, keepdims=True)
    xc = x - mu
    var = jnp.mean(xc * xc, axis=-1, keepdims=True)
    return xc * lax.rsqrt(var + LN_EPS) * g + b


def _rms(x):
    return x * lax.rsqrt(jnp.mean(x * x, axis=-1, keepdims=True) + RMS_EPS)


def _dot(a, b):
    return jnp.dot(a, b, preferred_element_type=F32)


def _dot_nt(a, b):
    return lax.dot_general(a, b, (((1,), (1,)), ((), ())), preferred_element_type=F32)


def _split_bf16(x):
    hi = x.astype(BF16)
    lo = (x - hi.astype(F32)).astype(BF16)
    return hi, lo


def _dot3(a, b, dot):
    ah, al = _split_bf16(a)
    bh, bl = _split_bf16(b)
    return dot(ah, bh) + dot(ah, bl) + dot(al, bh)


_IN_SPLITS = (D_ATTN, D_ATTN, D_ATTN, D_SSM, D_LRU, D_LRU)


def _project(hb, w_ref, out_refs):
    start = 0
    for o_ref, width in zip(out_refs, _IN_SPLITS):
        o_ref[...] = _dot(hb, w_ref[:, start:start + width])
        start += width


def _in_proj_kernel(h_ref, w_ref, *out_refs):
    _project(h_ref[...].astype(BF16), w_ref, out_refs)


def _ln_in_proj_kernel(x_ref, g_ref, b_ref, w_ref, h_ref, *out_refs):
    h = _layer_norm(x_ref[...], g_ref[...], b_ref[...])
    h_ref[...] = h
    _project(h.astype(BF16), w_ref, out_refs)


def _in_proj_call(h, w_in, ln=None):
    n, d = h.shape
    row = lambda w: pl.BlockSpec((ROW_TILE, w), lambda i: (i, 0))
    widths = _IN_SPLITS if ln is None else (d,) + _IN_SPLITS
    vecs = [] if ln is None else [v.reshape(1, d) for v in ln]
    return pl.pallas_call(
        _in_proj_kernel if ln is None else _ln_in_proj_kernel, grid=(n // ROW_TILE,),
        in_specs=[row(d)] + [_const_spec((1, d))] * len(vecs) + [_const_spec(w_in.shape)],
        out_specs=[row(w) for w in widths],
        out_shape=[jax.ShapeDtypeStruct((n, w), F32) for w in widths],
        compiler_params=_params("parallel"), name="in_proj")(h, *vecs, w_in)


AUG_ROWS = 16
GATE_LANE0 = 2
GATE_STRIDE = 32


def _moba_prepare(k_ref, v_ref, kt_scr, vb_scr, kmh_scr, kml_scr, nb):
    blk = MOBA_BLOCK
    width = k_ref.shape[1]
    arow = lax.broadcasted_iota(jnp.int32, (AUG_ROWS, blk), 0)
    akey = lax.broadcasted_iota(jnp.int32, (AUG_ROWS, blk), 1).astype(F32)
    lane_head = lax.shift_right_logical(lax.broadcasted_iota(jnp.int32, (1, width), 1),
                                        int(math.log2(ATTN_HEAD_DIM)))
    gate_rows = [jnp.zeros((GATE_LANE0, width), F32)]
    for j in range(nb):
        kj = k_ref[j * blk:(j + 1) * blk, :]
        ktj = kj.T.astype(BF16)
        aug = jnp.where(arow == 0, float(blk * j),
                        jnp.where(arow == 1, akey, jnp.where(arow == GATE_LANE0 + j, 1.0, 0.0)))
        aug = aug.astype(BF16)
        kt_scr[0, j] = ktj
        kt_scr[0, j, LANES:LANES + AUG_ROWS, :] = aug
        kt_scr[1, j] = ktj
        kt_scr[1, j, 0:AUG_ROWS, :] = aug
        vb_scr[j * blk:(j + 1) * blk, :] = v_ref[j * blk:(j + 1) * blk, :].astype(BF16)
        gate_rows.append(jnp.mean(kj, axis=0, keepdims=True))
    gate_rows.append(jnp.zeros((GATE_STRIDE - GATE_LANE0 - nb, width), F32))
    kmean = jnp.concatenate(gate_rows, axis=0)
    table = jnp.concatenate([jnp.where(lane_head == h, kmean, 0.0) for h in range(HEADS_PER_STEP)], axis=0)
    hi, lo = _split_bf16(table)
    kmh_scr[...] = hi
    kml_scr[...] = lo


def _moba_block(c, hg, slopes_ref, q_ref, o_ref, kt_scr, vb_scr, kmh_scr, kml_scr, nb):
    blk = MOBA_BLOCK
    width = q_ref.shape[1]
    q = q_ref[...]
    lane_head = lax.shift_right_logical(lax.broadcasted_iota(jnp.int32, (blk, width), 1),
                                        int(math.log2(ATTN_HEAD_DIM)))
    col = lax.broadcasted_iota(jnp.int32, (blk, LANES), 1)
    blockid = col - GATE_LANE0
    valid = (blockid >= 0) & (blockid < c)
    causal = (lax.broadcasted_iota(jnp.int32, (blk, blk), 0)
              >= lax.broadcasted_iota(jnp.int32, (blk, blk), 1))
    zeros = jnp.zeros((blk, LANES), F32)
    if c > MOBA_TOPK:
        qhi, qlo = _split_bf16(q)
        gate_all = (_dot_nt(qhi, kmh_scr[...]) + _dot_nt(qhi, kml_scr[...])
                    + _dot_nt(qlo, kmh_scr[...]))
    out = jnp.zeros((blk, width), F32)
    for hh in range(HEADS_PER_STEP):
        mine = lane_head == hh
        slope = slopes_ref[hg * HEADS_PER_STEP + hh]
        if c > MOBA_TOPK:
            gate = gate_all if hh == 0 else pltpu.roll(gate_all, LANES - GATE_STRIDE * hh, 1)
            gate = jnp.where(valid, gate, NEG_INF)
            rank = jnp.zeros((blk, LANES), jnp.int32)
            for jp in range(c):
                gj = gate[:, GATE_LANE0 + jp:GATE_LANE0 + jp + 1]
                beats = (gj > gate) | ((gj == gate) & (blockid > jp))
                rank = rank + beats.astype(jnp.int32)
            dropped = valid & (rank >= MOBA_TOPK)
        else:
            dropped = None
        extra = jnp.where(col < GATE_LANE0, slope, 0.0)
        if dropped is not None:
            extra = jnp.where(dropped, NEG_INF, extra)
        extra = jnp.concatenate([zeros, extra] if hh < HEADS_PER_STEP // 2 else [extra, zeros], axis=1)
        qa = jnp.where(mine, q * ATTN_HEAD_DIM ** -0.5, extra).astype(BF16)
        var = 0 if hh < HEADS_PER_STEP // 2 else 1
        tiles = [_dot(qa, kt_scr[var, j]) for j in range(c)]
        tiles.append(jnp.where(causal, _dot(qa, kt_scr[var, c]), NEG_INF))
        mx = tiles[0]
        for t in tiles[1:]:
            mx = jnp.maximum(mx, t)
        m = jnp.max(mx, axis=-1, keepdims=True)
        probs = [jnp.exp(t - m) for t in tiles]
        tot = probs[0]
        for p in probs[1:]:
            tot = tot + p
        l = jnp.sum(tot, axis=-1, keepdims=True)
        pb = jnp.concatenate([p.astype(BF16) for p in probs], axis=1)
        ctx = _dot(pb, vb_scr[0:(c + 1) * blk, :])
        out = jnp.where(mine, ctx / l, out)
    o_ref[...] = out


def _moba_kernel(slopes_ref, q_ref, k_ref, v_ref, o_ref, kt_scr, vb_scr, kmh_scr, kml_scr):
    hg = pl.program_id(1)
    i = pl.program_id(2)
    nb = vb_scr.shape[0] // MOBA_BLOCK

    @pl.when(i == 0)
    def _():
        _moba_prepare(k_ref, v_ref, kt_scr, vb_scr, kmh_scr, kml_scr, nb)

    for c in range(nb):
        @pl.when(i == c)
        def _():
            _moba_block(c, hg, slopes_ref, q_ref, o_ref, kt_scr, vb_scr, kmh_scr, kml_scr, nb)


def _moba_call(q, k, v, bsz, seq):
    nb = seq // MOBA_BLOCK
    width = HEADS_PER_STEP * ATTN_HEAD_DIM
    slopes = jnp.asarray([2.0 ** (-8.0 * (h + 1) / ATTN_HEADS) for h in range(ATTN_HEADS)], F32)
    qspec = pl.BlockSpec((MOBA_BLOCK, width), lambda b, g, i: (b * nb + i, g))
    kvspec = pl.BlockSpec((seq, width), lambda b, g, i: (b, g))
    return pl.pallas_call(
        _moba_kernel, grid=(bsz, D_ATTN // width, nb),
        in_specs=[pl.BlockSpec(memory_space=pltpu.SMEM), qspec, kvspec, kvspec],
        out_specs=qspec, out_shape=jax.ShapeDtypeStruct(q.shape, F32),
        scratch_shapes=[pltpu.VMEM((2, nb, width, MOBA_BLOCK), BF16),
                        pltpu.VMEM((seq, width), BF16),
                        pltpu.VMEM((LANES, width), BF16),
                        pltpu.VMEM((LANES, width), BF16)],
        compiler_params=_params("parallel", "parallel", "arbitrary"), name="moba")(slopes, q, k, v)


def _s5_kernel(u_ref, bmat_ref, apow_ref, cmat_ref, d_ref, wglu_ref, bglu_ref, o_ref,
               half_scr, up_scr, x_scr, xb_scr, carry_scr):
    c = pl.program_id(1)
    chunk = u_ref.shape[0]
    steps = chunk // SUBLANES
    ns = D_STATE

    @pl.when(c == 0)
    def _():
        carry_scr[...] = jnp.zeros_like(carry_scr)

    halves = [slice(k * LANES, (k + 1) * LANES) for k in range(u_ref.shape[1] // LANES)]
    for k, cols in enumerate(halves):
        half_scr[k] = u_ref[:, cols]
        for tau in range(steps):
            up_scr[tau * SUBLANES:(tau + 1) * SUBLANES, cols] = (
                half_scr[k, pl.ds(tau, SUBLANES, stride=steps), :])
    up = up_scr[...]
    x_scr[...] = _dot(up.astype(BF16), bmat_ref[...])

    a1r = jnp.broadcast_to(apow_ref[0:1, 0:ns], (SUBLANES, ns))
    a1i = jnp.broadcast_to(apow_ref[0:1, ns:2 * ns], (SUBLANES, ns))

    def local_step(tau, state):
        xr, xi = state
        rows = pl.ds(pl.multiple_of(tau * SUBLANES, SUBLANES), SUBLANES)
        xr, xi = (a1r * xr - a1i * xi + x_scr[rows, 0:ns],
                  a1r * xi + a1i * xr + x_scr[rows, ns:2 * ns])
        x_scr[rows, 0:ns] = xr
        x_scr[rows, ns:2 * ns] = xi
        return xr, xi

    zero = jnp.zeros((SUBLANES, ns), F32)
    end_r, end_i = lax.fori_loop(0, steps, local_step, (zero, zero), unroll=4)

    alr = apow_ref[steps - 1:steps, 0:ns]
    ali = apow_ref[steps - 1:steps, ns:2 * ns]
    er = carry_scr[:, 0:ns]
    ei = carry_scr[:, ns:2 * ns]
    enter_r, enter_i = [], []
    for s in range(SUBLANES):
        enter_r.append(er)
        enter_i.append(ei)
        er, ei = (end_r[s:s + 1, :] + alr * er - ali * ei,
                  end_i[s:s + 1, :] + alr * ei + ali * er)
    carry_scr[:, 0:ns] = er
    carry_scr[:, ns:2 * ns] = ei
    cr = jnp.concatenate(enter_r, axis=0)
    ci = jnp.concatenate(enter_i, axis=0)

    def carry_step(pair, _):
        halves_r, halves_i = [], []
        for k in range(2):
            tau = 2 * pair + k
            rows = pl.ds(pl.multiple_of(tau * SUBLANES, SUBLANES), SUBLANES)
            pr = jnp.broadcast_to(apow_ref[pl.ds(tau, 1), 0:ns], (SUBLANES, ns))
            pi = jnp.broadcast_to(apow_ref[pl.ds(tau, 1), ns:2 * ns], (SUBLANES, ns))
            halves_r.append(x_scr[rows, 0:ns] + pr * cr - pi * ci)
            halves_i.append(x_scr[rows, ns:2 * ns] + pr * ci + pi * cr)
        rows2 = pl.ds(pl.multiple_of(pair * 2 * SUBLANES, 2 * SUBLANES), 2 * SUBLANES)
        xb_scr[rows2, 0:ns] = jnp.concatenate(halves_r, axis=0).astype(BF16)
        xb_scr[rows2, ns:2 * ns] = jnp.concatenate(halves_i, axis=0).astype(BF16)
        return 0

    lax.fori_loop(0, steps // 2, carry_step, 0, unroll=2)

    y = _dot(xb_scr[...], cmat_ref[...])
    y = jax.nn.gelu(y + d_ref[...] * up)
    gate = jax.nn.sigmoid(_dot(y.astype(BF16), wglu_ref[...]) + bglu_ref[...])
    up_scr[...] = y * gate
    for k, cols in enumerate(halves):
        for tau in range(steps):
            half_scr[k, pl.ds(tau, SUBLANES, stride=steps), :] = (
                up_scr[tau * SUBLANES:(tau + 1) * SUBLANES, cols])
        o_ref[:, cols] = half_scr[k]


def _complex_mul(ar, ai, br, bi):
    return ar * br - ai * bi, ar * bi + ai * br


def _s5_tables(a_re, a_im, b_re, b_im, c_re, c_im, log_dt):
    g, p, h = b_re.shape
    dt = jnp.exp(log_dt)[:, None]
    mag = jnp.exp(dt * a_re)
    ab_re, ab_im = mag * jnp.cos(dt * a_im), mag * jnp.sin(dt * a_im)
    den = a_re * a_re + a_im * a_im
    nr, ni = ab_re - 1.0, ab_im
    f_re = (nr * a_re + ni * a_im) / den
    f_im = (ni * a_re - nr * a_im) / den
    bb_re = f_re[..., None] * b_re - f_im[..., None] * b_im
    bb_im = f_re[..., None] * b_im + f_im[..., None] * b_re
    eye = jnp.eye(g, dtype=F32)
    bmat_re = jnp.einsum('gph,gk->ghkp', bb_re, eye).reshape(g * h, g * p)
    bmat_im = jnp.einsum('gph,gk->ghkp', bb_im, eye).reshape(g * h, g * p)
    bmat = jnp.concatenate([bmat_re, bmat_im], axis=1).astype(BF16)
    cmat_re = jnp.einsum('ghp,gk->gpkh', c_re, eye).reshape(g * p, g * h)
    cmat_im = jnp.einsum('ghp,gk->gpkh', c_im, eye).reshape(g * p, g * h)
    cmat = jnp.concatenate([cmat_re, -cmat_im], axis=0).astype(BF16)
    pr, pi = ab_re.reshape(1, g * p), ab_im.reshape(1, g * p)
    steps = SCAN_CHUNK // SUBLANES
    while pr.shape[0] < steps:
        nr2, ni2 = _complex_mul(pr, pi, pr[-1:], pi[-1:])
        pr, pi = jnp.concatenate([pr, nr2], axis=0), jnp.concatenate([pi, ni2], axis=0)
    apow = jnp.concatenate([pr[:steps], pi[:steps]], axis=1)
    return bmat, apow, cmat


def _s5_call(u, tables, d_skip, w_glu, b_glu, bsz, seq):
    bmat, apow, cmat = tables
    n, d = u.shape
    nc = seq // SCAN_CHUNK
    row = pl.BlockSpec((SCAN_CHUNK, d), lambda b, c: (b * nc + c, 0))
    return pl.pallas_call(
        _s5_kernel, grid=(bsz, nc),
        in_specs=[row, _const_spec(bmat.shape), _const_spec(apow.shape),
                  _const_spec(cmat.shape), _const_spec((1, d)), _const_spec((d, d)),
                  _const_spec((1, d))],
        out_specs=row, out_shape=jax.ShapeDtypeStruct((n, d), F32),
        scratch_shapes=[pltpu.VMEM((d // LANES, SCAN_CHUNK, LANES), F32),
                        pltpu.VMEM((SCAN_CHUNK, d), F32),
                        pltpu.VMEM((SCAN_CHUNK, 2 * D_STATE), F32),
                        pltpu.VMEM((SCAN_CHUNK, 2 * D_STATE), BF16),
                        pltpu.VMEM((1, 2 * D_STATE), F32)],
        compiler_params=_params("parallel", "arbitrary"), name="s5")(
            u, bmat, apow, cmat, d_skip.reshape(1, d), w_glu.astype(BF16), b_glu.reshape(1, d))


def _lru_kernel(xl_ref, gl_ref, cw_ref, cb_ref, wa_ref, ba_ref, wx_ref, bx_ref, lam_ref, o_ref,
                ext_scr, a_scr, h_scr, carry_scr):
    c = pl.program_id(1)
    chunk = xl_ref.shape[0]
    halo = SUBLANES

    @pl.when(c == 0)
    def _():
        ext_scr[0:halo, :] = jnp.zeros((halo, ext_scr.shape[1]), F32)
        carry_scr[...] = jnp.zeros_like(carry_scr)

    xl = xl_ref[...]
    ext_scr[halo:, :] = xl
    ext = ext_scr[...]
    xc = cb_ref[...] + cw_ref[CONV_WIDTH - 1:CONV_WIDTH, :] * xl
    for back in range(1, CONV_WIDTH):
        shifted = pltpu.roll(ext, back, 0)[halo:, :]
        xc = xc + cw_ref[CONV_WIDTH - 1 - back:CONV_WIDTH - back, :] * shifted
    ext_scr[0:halo, :] = xl[chunk - halo:, :]

    xb = xc.astype(BF16)
    r = jax.nn.sigmoid(_dot(xb, wa_ref[...]) + ba_ref[...])
    gate_in = jax.nn.sigmoid(_dot(xb, wx_ref[...]) + bx_ref[...])
    neg_lam = -lam_ref[...]
    softplus = jnp.maximum(neg_lam, 0.0) + jnp.log1p(jnp.exp(-jnp.abs(neg_lam)))
    log_a = -LRU_C * r * softplus
    a = jnp.exp(log_a)
    one_minus_a2 = jnp.tanh(-log_a) * (1.0 + a * a)
    a_scr[...] = a
    h_scr[...] = jnp.sqrt(one_minus_a2) * (gate_in * xc)

    sub = lax.broadcasted_iota(jnp.int32, (SUBLANES, a.shape[1]), 0)

    def tile_body(t, carry):
        rows = pl.ds(pl.multiple_of(t * SUBLANES, SUBLANES), SUBLANES)
        at = a_scr[rows, :]
        bt = h_scr[rows, :]
        for lvl in range(3):
            keep = sub >= (1 << lvl)
            a_sh = jnp.where(keep, pltpu.roll(at, 1 << lvl, 0), 1.0)
            b_sh = jnp.where(keep, pltpu.roll(bt, 1 << lvl, 0), 0.0)
            bt = bt + at * b_sh
            at = at * a_sh
        ht = bt + at * carry
        h_scr[rows, :] = ht
        return ht[SUBLANES - 1:SUBLANES, :]

    carry_scr[...] = lax.fori_loop(0, chunk // SUBLANES, tile_body, carry_scr[...], unroll=2)
    o_ref[...] = h_scr[...] * jax.nn.gelu(gl_ref[...])


def _block_diag(w):
    heads, di, do = w.shape
    eye = jnp.eye(heads, dtype=w.dtype)
    return jnp.einsum('hij,hk->hikj', w, eye).reshape(heads * di, heads * do)


def _lru_call(xl, gl, conv_w, conv_b, w_a, b_a, w_x, b_x, lam, bsz, seq):
    n, d = xl.shape
    nc = seq // SCAN_CHUNK
    row = pl.BlockSpec((SCAN_CHUNK, d), lambda b, c: (b * nc + c, 0))
    vec = _const_spec((1, d))
    mat = _const_spec((d, d))
    return pl.pallas_call(
        _lru_kernel, grid=(bsz, nc),
        in_specs=[row, row, _const_spec((CONV_WIDTH, d)), vec, mat, vec, mat, vec, vec],
        out_specs=row, out_shape=jax.ShapeDtypeStruct((n, d), F32),
        scratch_shapes=[pltpu.VMEM((SCAN_CHUNK + SUBLANES, d), F32),
                        pltpu.VMEM((SCAN_CHUNK, d), F32),
                        pltpu.VMEM((SCAN_CHUNK, d), F32),
                        pltpu.VMEM((1, d), F32)],
        compiler_params=_params("parallel", "arbitrary"), name="rglru")(
            xl, gl, conv_w, conv_b.reshape(1, d), _block_diag(w_a).astype(BF16), b_a.reshape(1, d),
            _block_diag(w_x).astype(BF16), b_x.reshape(1, d), lam.reshape(1, d))


def _out_proj_kernel(ya_ref, ys_ref, yl_ref, h_ref, mg_ref, w_ref, g_ref, b_ref, o_ref):
    mix = None
    start = 0
    for y_ref in (ya_ref, ys_ref, yl_ref):
        width = y_ref.shape[1]
        y = (_rms(y_ref[...]) * mg_ref[:, start:start + width]).astype(BF16)
        part = _dot(y, w_ref[start:start + width, :])
        mix = part if mix is None else mix + part
        start += width
    o_ref[...] = _layer_norm(ALPHA * h_ref[...] + mix, g_ref[...], b_ref[...])


def _out_proj_call(ya, ys, yl, h, mix_g, w_out, g, b):
    n, d = h.shape
    row = lambda w: pl.BlockSpec((ROW_TILE, w), lambda i: (i, 0))
    vec = _const_spec((1, d))
    return pl.pallas_call(
        _out_proj_kernel, grid=(n // ROW_TILE,),
        in_specs=[row(ya.shape[1]), row(ys.shape[1]), row(yl.shape[1]), row(d), vec,
                  _const_spec(w_out.shape), vec, vec],
        out_specs=row(d), out_shape=jax.ShapeDtypeStruct((n, d), F32),
        compiler_params=_params("parallel"), name="out_proj")(
            ya, ys, yl, h, mix_g.reshape(1, d), w_out, g.reshape(1, d), b.reshape(1, d))


def _mem_kv_kernel(mem_ref, wk_ref, wv_ref, kt_ref, v_ref):
    mb = mem_ref[...].astype(BF16)
    kt_ref[0] = _dot(mb, wk_ref[...]).T.astype(BF16)
    v_ref[0] = _dot(mb, wv_ref[...]).astype(BF16)


def _mem_kv_call(mem, wk, wv):
    bsz, m, d = mem.shape
    return pl.pallas_call(
        _mem_kv_kernel, grid=(bsz,),
        in_specs=[pl.BlockSpec((m, d), lambda b: (b, 0)), _const_spec((d, d)), _const_spec((d, d))],
        out_specs=[pl.BlockSpec((1, d, m), lambda b: (b, 0, 0)),
                   pl.BlockSpec((1, m, d), lambda b: (b, 0, 0))],
        out_shape=[jax.ShapeDtypeStruct((bsz, d, m), BF16), jax.ShapeDtypeStruct((bsz, m, d), BF16)],
        compiler_params=_params("parallel"), name="mem_kv")(mem.reshape(bsz * m, d), wk, wv)


def _cross_kernel(h_ref, kt_ref, v_ref, wq_ref, wo_ref, g_ref, b_ref, wr_ref, br_ref,
                  haug_ref, route_ref, count_ref, ctx_scr, run_scr):
    @pl.when(pl.program_id(0) == 0)
    def _():
        run_scr[...] = jnp.zeros_like(run_scr)

    h = h_ref[...]
    d = h.shape[1]
    hd = d // MEM_HEADS
    q = _dot(h.astype(BF16), wq_ref[...])
    scale = hd ** -0.5
    for hh in range(MEM_HEADS):
        cols = slice(hh * hd, (hh + 1) * hd)
        s = _dot(q[:, cols].astype(BF16), kt_ref[0, cols, :]) * scale
        s = s - jnp.max(s, axis=-1, keepdims=True)
        p = jnp.exp(s)
        p = p / jnp.sum(p, axis=-1, keepdims=True)
        ctx_scr[:, cols] = _dot(p.astype(BF16), v_ref[0, :, cols]).astype(BF16)
    cross = _dot(ctx_scr[...], wo_ref[...])
    h2 = _layer_norm(ALPHA * h + cross, g_ref[...], b_ref[...])

    rows = h2.shape[0]
    comb, g_idx = _route(h2, wr_ref, br_ref)
    lane = lax.broadcasted_iota(jnp.int32, (rows, LANES), 1)
    in_my_group = lane == g_idx
    onehot = jnp.where(in_my_group, 1.0, 0.0).astype(BF16)
    tri = jnp.where(lax.broadcasted_iota(jnp.int32, (rows, rows), 0)
                    >= lax.broadcasted_iota(jnp.int32, (rows, rows), 1), 1.0, 0.0).astype(BF16)
    seen = _dot(tri, onehot) + run_scr[...]
    rank = jnp.sum(jnp.where(in_my_group, seen - 1.0, 0.0), axis=-1, keepdims=True)
    run_scr[...] = seen[rows - 1:rows, :]
    route = jnp.where(lane == ROUTE_GROUP_LANE, g_idx.astype(F32),
                      jnp.where(lane == ROUTE_RANK_LANE, rank, comb))
    haug_ref[:, 0:d] = h2
    haug_ref[:, d:d + LANES] = route
    route_ref[...] = route
    count_ref[...] = jnp.broadcast_to(seen[rows - 1:rows, :], count_ref.shape)


def _cross_call(h, kt, v, wq, wo, g, b, wr, br, seq):
    n, d = h.shape
    m = v.shape[1]
    tiles_per_seq = seq // ROW_TILE
    row = lambda w: pl.BlockSpec((ROW_TILE, w), lambda i: (i, 0))
    vec = _const_spec((1, d))
    return pl.pallas_call(
        _cross_kernel, grid=(n // ROW_TILE,),
        in_specs=[row(d), pl.BlockSpec((1, d, m), lambda i: (i // tiles_per_seq, 0, 0)),
                  pl.BlockSpec((1, m, d), lambda i: (i // tiles_per_seq, 0, 0)),
                  _const_spec((d, d)), _const_spec((d, d)), vec, vec,
                  _const_spec(wr.shape), _const_spec(br.shape)],
        out_specs=[row(d + LANES), row(LANES), _const_spec((SUBLANES, LANES))],
        out_shape=[jax.ShapeDtypeStruct((n, d + LANES), F32), jax.ShapeDtypeStruct((n, LANES), F32),
                   jax.ShapeDtypeStruct((SUBLANES, LANES), F32)],
        scratch_shapes=[pltpu.VMEM((ROW_TILE, d), BF16), pltpu.VMEM((1, LANES), F32)],
        compiler_params=_params("arbitrary"), name="cross_attn")(
            h, kt, v, wq, wo, g.reshape(1, d), b.reshape(1, d), wr, br)


def _first_max(x, valid, lane):
    xm = jnp.where(valid, x, -jnp.inf)
    mx = jnp.max(xm, axis=-1, keepdims=True)
    idx = jnp.min(jnp.where(valid & (xm == mx), lane, LANES), axis=-1, keepdims=True)
    return xm, mx, idx


def _route(h, wr_ref, br_ref):
    rows = h.shape[0]
    logits = _dot3(h, wr_ref[...], _dot) + br_ref[...]
    lane = lax.broadcasted_iota(jnp.int32, (rows, LANES), 1)
    g_logits, g_max, g_idx = _first_max(logits, lane < N_GROUPS, lane)
    g_w = 1.0 / jnp.sum(jnp.exp(g_logits - g_max), axis=-1, keepdims=True)
    expert = lane - N_GROUPS
    group_of = lax.shift_right_arithmetic(expert, int(math.log2(EXPERTS_PER_GROUP)))
    in_group = (expert >= 0) & (expert < N_EXPERTS) & (group_of == g_idx)
    e_logits, e_max, _ = _first_max(logits, in_group, lane)
    e_exp = jnp.exp(e_logits - e_max)
    e_prob = e_exp / jnp.sum(e_exp, axis=-1, keepdims=True)
    _, p1, i1 = _first_max(e_prob, in_group, lane)
    _, p2, i2 = _first_max(e_prob, in_group & (lane != i1), lane)
    denom = p1 + p2
    local = jnp.where(lane == i1, p1 / denom, jnp.where(lane == i2, p2 / denom, 0.0))
    return local * g_w, g_idx


def _router_tables(wr_g, br_g, wr_e, br_e):
    d = wr_g.shape[0]
    pad = LANES - N_GROUPS - N_EXPERTS
    wr = jnp.concatenate([wr_g, wr_e, jnp.zeros((d, pad), F32)], axis=1)
    br = jnp.concatenate([br_g, br_e, jnp.zeros((pad,), F32)]).reshape(1, LANES)
    return wr, br


def _permute_kernel(scatter, pos_ref, src_ref, dst_ref, sem):
    base = pl.program_id(0) * PERM_ROWS

    def row_copy(r):
        here = base + r
        there = pos_ref[here]
        src_row, dst_row = (here, there) if scatter else (there, here)
        return pltpu.make_async_copy(src_ref.at[pl.ds(src_row, 1), :], dst_ref.at[pl.ds(dst_row, 1), :], sem)

    def start(r, carry):
        row_copy(r).start()
        return carry

    def wait(r, carry):
        row_copy(r).wait()
        return carry

    lax.fori_loop(0, PERM_ROWS, start, 0, unroll=8)
    lax.fori_loop(0, PERM_ROWS, wait, 0, unroll=8)


def _permute_call(src, pos, scatter, name):
    n, w = src.shape
    grid_spec = pltpu.PrefetchScalarGridSpec(
        num_scalar_prefetch=1, grid=(n // PERM_ROWS,),
        in_specs=[pl.BlockSpec(memory_space=pl.ANY)], out_specs=pl.BlockSpec(memory_space=pl.ANY),
        scratch_shapes=[pltpu.SemaphoreType.DMA(())])
    return pl.pallas_call(
        functools.partial(_permute_kernel, scatter), grid_spec=grid_spec,
        out_shape=jax.ShapeDtypeStruct((n, w), src.dtype),
        compiler_params=_params("arbitrary"), name=name)(pos, src)


def _moe_kernel(tile_ref, group_ref, flag_ref, x_ref, wg_ref, wu_ref, wd_ref, g_ref, b_ref, o_ref,
                hb_scr, acc_scr):
    p = pl.program_id(0)
    e = pl.program_id(1)
    d = o_ref.shape[1]
    flags = flag_ref[p]

    @pl.when(((flags & PAIR_FIRST) != 0) & (e == 0))
    def _():
        hb_scr[...] = x_ref[:, 0:d].astype(BF16)
        acc_scr[...] = jnp.zeros_like(acc_scr)

    @pl.when((flags & PAIR_VALID) != 0)
    def _():
        hb = hb_scr[...]
        he = jax.nn.silu(_dot(hb, wg_ref[0])) * _dot(hb, wu_ref[0])
        y = _dot(he.astype(BF16), wd_ref[0])
        route = x_ref[:, d:d + LANES]
        lane = lax.broadcasted_iota(jnp.int32, route.shape, 1)
        mine = lane == N_GROUPS + group_ref[p] * EXPERTS_PER_GROUP + e
        weight = jnp.sum(jnp.where(mine, route, 0.0), axis=-1, keepdims=True)
        acc_scr[...] += weight * y

    @pl.when(((flags & PAIR_LAST) != 0) & (e == pl.num_programs(1) - 1))
    def _():
        o_ref[...] = _layer_norm(ALPHA * x_ref[:, 0:d] + acc_scr[...], g_ref[...], b_ref[...])


def _moe_call(xs, pair_tile, pair_group, pair_flags, w_gate, w_up, w_down, g, b):
    n, daug = xs.shape
    d = daug - LANES
    _, _, de = w_gate.shape
    last_e = EXPERTS_PER_GROUP - 1

    def expert(p, e, tile, group, flags):
        return group[p] * EXPERTS_PER_GROUP + jnp.where((flags[p] & PAIR_VALID) != 0, e, last_e)

    vec = pl.BlockSpec((1, d), lambda p, e, tile, group, flags: (0, 0))
    grid_spec = pltpu.PrefetchScalarGridSpec(
        num_scalar_prefetch=3, grid=(pair_tile.shape[0], EXPERTS_PER_GROUP),
        in_specs=[pl.BlockSpec((ROW_TILE, daug), lambda p, e, tile, group, flags: (tile[p], 0)),
                  pl.BlockSpec((1, d, de), lambda p, e, *s: (expert(p, e, *s), 0, 0)),
                  pl.BlockSpec((1, d, de), lambda p, e, *s: (expert(p, e, *s), 0, 0)),
                  pl.BlockSpec((1, de, d), lambda p, e, *s: (expert(p, e, *s), 0, 0)), vec, vec],
        out_specs=pl.BlockSpec((ROW_TILE, d), lambda p, e, tile, group, flags: (tile[p], 0)),
        scratch_shapes=[pltpu.VMEM((ROW_TILE, d), BF16), pltpu.VMEM((ROW_TILE, d), F32)])
    return pl.pallas_call(
        _moe_kernel, grid_spec=grid_spec, out_shape=jax.ShapeDtypeStruct((n, d), F32),
        compiler_params=_params("arbitrary", "arbitrary"), name="moe")(
            pair_tile, pair_group, pair_flags, xs, w_gate, w_up, w_down, g.reshape(1, d), b.reshape(1, d))


def _moe_tables(route, counts, n):
    gid = route[:, ROUTE_GROUP_LANE].astype(jnp.int32)
    rank = route[:, ROUTE_RANK_LANE].astype(jnp.int32)
    cnt = counts[0, :N_GROUPS].astype(jnp.int32)
    ends = jnp.cumsum(cnt)
    starts = ends - cnt
    pos = starts[gid] + rank
    tiles = n // ROW_TILE
    pairs = tiles + N_GROUPS - 1
    lo = jnp.arange(tiles, dtype=jnp.int32)[:, None] * ROW_TILE
    active = ((starts[None, :] < lo + ROW_TILE) & (ends[None, :] > lo)).reshape(-1)
    n_active = jnp.sum(active.astype(jnp.int32))
    idx = jnp.nonzero(active, size=pairs, fill_value=0)[0].astype(jnp.int32)
    valid = jnp.arange(pairs, dtype=jnp.int32) < n_active
    idx = jnp.where(valid, idx, idx[n_active - 1])
    tile = idx // N_GROUPS
    group = idx % N_GROUPS
    prev_tile = jnp.concatenate([jnp.full((1,), -1, jnp.int32), tile[:-1]])
    next_tile = jnp.concatenate([tile[1:], jnp.full((1,), -1, jnp.int32)])
    next_valid = jnp.concatenate([valid[1:], jnp.zeros((1,), bool)])
    first = valid & (tile != prev_tile)
    last = valid & ((tile != next_tile) | ~next_valid)
    flags = (valid.astype(jnp.int32) * PAIR_VALID + first.astype(jnp.int32) * PAIR_FIRST
             + last.astype(jnp.int32) * PAIR_LAST)
    return pos, tile, group, flags


def _moe_full(haug, route, counts, w_gate, w_up, w_down, g, b):
    n = haug.shape[0]
    pos, tile, group, flags = _moe_tables(route, counts, n)
    xs = _permute_call(haug, pos, True, "moe_dispatch")
    ys = _moe_call(xs, tile, group, flags, w_gate.astype(BF16), w_up.astype(BF16),
                   w_down.astype(BF16), g, b)
    return _permute_call(ys, pos, False, "moe_return")


def kernel(x, mem, ln0_g, ln0_b, w_in, mix_g, w_out, ssm_a_re, ssm_a_im, ssm_b_re, ssm_b_im, ssm_c_re, ssm_c_im, ssm_d, ssm_log_dt, ssm_w_glu, ssm_b_glu, lru_conv_w, lru_conv_b, lru_w_a, lru_b_a, lru_w_x, lru_b_x, lru_lam, ln1_g, ln1_b, mem_wq, mem_wk, mem_wv, mem_wo, ln2_g, ln2_b, moe_wr_g, moe_br_g, moe_wr_e, moe_br_e, moe_w_gate, moe_w_up, moe_w_down, ln3_g, ln3_b):
    bsz, seq, d = x.shape
    depth = w_in.shape[0]
    assert seq % MOBA_BLOCK == 0 and seq % SCAN_CHUNK == 0 and seq % ROW_TILE == 0
    assert seq // MOBA_BLOCK <= SUBLANES
    h = x.reshape(bsz * seq, d)
    for l in range(depth):
        if l == 0:
            h, q, k, v, u, xl, gl = _in_proj_call(h, w_in[l].astype(BF16), ln=(ln0_g, ln0_b))
        else:
            q, k, v, u, xl, gl = _in_proj_call(h, w_in[l].astype(BF16))
        y_attn = _moba_call(q, k, v, bsz, seq)
        tables = _s5_tables(ssm_a_re[l], ssm_a_im[l], ssm_b_re[l], ssm_b_im[l], ssm_c_re[l],
                            ssm_c_im[l], ssm_log_dt[l])
        y_ssm = _s5_call(u, tables, ssm_d[l], ssm_w_glu[l], ssm_b_glu[l], bsz, seq)
        y_lru = _lru_call(xl, gl, lru_conv_w[l], lru_conv_b[l], lru_w_a[l], lru_b_a[l],
                          lru_w_x[l], lru_b_x[l], lru_lam[l], bsz, seq)
        h = _out_proj_call(y_attn, y_ssm, y_lru, h, mix_g[l], w_out[l].astype(BF16),
                           ln1_g[l], ln1_b[l])
        kt, vm = _mem_kv_call(mem, mem_wk[l].astype(BF16), mem_wv[l].astype(BF16))
        wr, br = _router_tables(moe_wr_g[l], moe_br_g[l], moe_wr_e[l], moe_br_e[l])
        haug, route, counts = _cross_call(h, kt, vm, mem_wq[l].astype(BF16), mem_wo[l].astype(BF16),
                                          ln2_g[l], ln2_b[l], wr, br, seq)
        h = _moe_full(haug, route, counts, moe_w_gate[l], moe_w_up[l], moe_w_down[l],
                      ln3_g[l], ln3_b[l])
    return h.reshape(bsz, seq, d)
```

```python
import functools
import math

import jax
import jax.numpy as jnp
from jax import lax
from jax.experimental import pallas as pl
from jax.experimental.pallas import tpu as pltpu

F32 = jnp.float32
BF16 = jnp.bfloat16

D_ATTN = 512
D_SSM = 256
D_LRU = 256
SSM_GROUPS = 16
SSM_GROUP = 16
SSM_STATE = 64
D_STATE = SSM_GROUPS * SSM_STATE
LRU_HEADS = 4
CONV_WIDTH = 4
LRU_C = 8.0
ATTN_HEADS = 8
ATTN_HEAD_DIM = 64
HEADS_PER_STEP = 4
MOBA_BLOCK = 256
MOBA_TOPK = 3
MEM_HEADS = 4
N_GROUPS = 4
EXPERTS_PER_GROUP = 4
N_EXPERTS = 16
DEPTH = 2
ALPHA = (2.0 * DEPTH) ** 0.25
LN_EPS = 1e-5
RMS_EPS = 1e-6
NEG_INF = -1e30

SUBLANES = 8
LANES = 128
VMEM_LIMIT = 48 * 1024 * 1024
ROW_TILE = 512
MOE_TILE = 1024
PERM_ROWS = 512
PERM_UNROLL = 16
ROUTE_GROUP_LANE = 0
ROUTE_RANK_LANE = 1
PAIR_VALID, PAIR_FIRST, PAIR_LAST = 1, 2, 4
SCAN_CHUNK = 512


def _params(*sem):
    return pltpu.CompilerParams(dimension_semantics=sem, vmem_limit_bytes=VMEM_LIMIT)


def _const_spec(shape):
    zeros = (0,) * len(shape)
    return pl.BlockSpec(shape, lambda *_: zeros)


def _layer_norm(x, g, b):
    mu = jnp.mean(x, axis=-1, keepdims=True)
    xc = x - mu
    var = jnp.mean(xc * xc, axis=-1, keepdims=True)
    return xc * lax.rsqrt(var + LN_EPS) * g + b


def _rms(x):
    return x * lax.rsqrt(jnp.mean(x * x, axis=-1, keepdims=True) + RMS_EPS)


def _dot(a, b):
    return jnp.dot(a, b, preferred_element_type=F32)


def _dot_nt(a, b):
    return lax.dot_general(a, b, (((1,), (1,)), ((), ())), preferred_element_type=F32)


def _split_bf16(x):
    hi = x.astype(BF16)
    lo = (x - hi.astype(F32)).astype(BF16)
    return hi, lo


def _dot3(a, b, dot):
    ah, al = _split_bf16(a)
    bh, bl = _split_bf16(b)
    return dot(ah, bh) + dot(ah, bl) + dot(al, bh)


_IN_SPLITS = (D_ATTN, D_ATTN, D_ATTN, D_SSM, D_LRU, D_LRU)


def _project(hb, w_ref, out_refs):
    start = 0
    for o_ref, width in zip(out_refs, _IN_SPLITS):
        o_ref[...] = _dot(hb, w_ref[:, start:start + width])
        start += width


def _in_proj_kernel(h_ref, w_ref, *out_refs):
    _project(h_ref[...].astype(BF16), w_ref, out_refs)


def _ln_in_proj_kernel(x_ref, g_ref, b_ref, w_ref, h_ref, *out_refs):
    h = _layer_norm(x_ref[...], g_ref[...], b_ref[...])
    h_ref[...] = h
    _project(h.astype(BF16), w_ref, out_refs)


def _in_proj_call(h, w_in, ln=None):
    n, d = h.shape
    row = lambda w: pl.BlockSpec((ROW_TILE, w), lambda i: (i, 0))
    widths = _IN_SPLITS if ln is None else (d,) + _IN_SPLITS
    vecs = [] if ln is None else [v.reshape(1, d) for v in ln]
    return pl.pallas_call(
        _in_proj_kernel if ln is None else _ln_in_proj_kernel, grid=(n // ROW_TILE,),
        in_specs=[row(d)] + [_const_spec((1, d))] * len(vecs) + [_const_spec(w_in.shape)],
        out_specs=[row(w) for w in widths],
        out_shape=[jax.ShapeDtypeStruct((n, w), F32) for w in widths],
        compiler_params=_params("parallel"), name="in_proj")(h, *vecs, w_in)


AUG_ROWS = 16
GATE_LANE0 = 2
GATE_STRIDE = 32


def _moba_prepare(k_ref, v_ref, kt_scr, vb_scr, kmh_scr, kml_scr, nb):
    blk = MOBA_BLOCK
    width = k_ref.shape[1]
    arow = lax.broadcasted_iota(jnp.int32, (AUG_ROWS, blk), 0)
    akey = lax.broadcasted_iota(jnp.int32, (AUG_ROWS, blk), 1).astype(F32)
    lane_head = lax.shift_right_logical(lax.broadcasted_iota(jnp.int32, (1, width), 1),
                                        int(math.log2(ATTN_HEAD_DIM)))
    gate_rows = [jnp.zeros((GATE_LANE0, width), F32)]
    for j in range(nb):
        kj = k_ref[j * blk:(j + 1) * blk, :]
        ktj = kj.T.astype(BF16)
        aug = jnp.where(arow == 0, float(blk * j),
                        jnp.where(arow == 1, akey, jnp.where(arow == GATE_LANE0 + j, 1.0, 0.0)))
        aug = aug.astype(BF16)
        kt_scr[0, j] = ktj
        kt_scr[0, j, LANES:LANES + AUG_ROWS, :] = aug
        kt_scr[1, j] = ktj
        kt_scr[1, j, 0:AUG_ROWS, :] = aug
        vb_scr[j * blk:(j + 1) * blk, :] = v_ref[j * blk:(j + 1) * blk, :].astype(BF16)
        gate_rows.append(jnp.mean(kj, axis=0, keepdims=True))
    gate_rows.append(jnp.zeros((GATE_STRIDE - GATE_LANE0 - nb, width), F32))
    kmean = jnp.concatenate(gate_rows, axis=0)
    table = jnp.concatenate([jnp.where(lane_head == h, kmean, 0.0) for h in range(HEADS_PER_STEP)], axis=0)
    hi, lo = _split_bf16(table)
    kmh_scr[...] = hi
    kml_scr[...] = lo


def _moba_block(c, hg, slopes_ref, q_ref, o_ref, kt_scr, vb_scr, kmh_scr, kml_scr, nb):
    blk = MOBA_BLOCK
    width = q_ref.shape[1]
    q = q_ref[...]
    lane_head = lax.shift_right_logical(lax.broadcasted_iota(jnp.int32, (blk, width), 1),
                                        int(math.log2(ATTN_HEAD_DIM)))
    col = lax.broadcasted_iota(jnp.int32, (blk, LANES), 1)
    blockid = col - GATE_LANE0
    valid = (blockid >= 0) & (blockid < c)
    causal = (lax.broadcasted_iota(jnp.int32, (blk, blk), 0)
              >= lax.broadcasted_iota(jnp.int32, (blk, blk), 1))
    zeros = jnp.zeros((blk, LANES), F32)
    if c > MOBA_TOPK:
        qhi, qlo = _split_bf16(q)
        gate_all = (_dot_nt(qhi, kmh_scr[...]) + _dot_nt(qhi, kml_scr[...])
                    + _dot_nt(qlo, kmh_scr[...]))
    out = jnp.zeros((blk, width), F32)
    for hh in range(HEADS_PER_STEP):
        mine = lane_head == hh
        slope = slopes_ref[hg * HEADS_PER_STEP + hh]
        if c > MOBA_TOPK:
            gate = gate_all if hh == 0 else pltpu.roll(gate_all, LANES - GATE_STRIDE * hh, 1)
            gate = jnp.where(valid, gate, NEG_INF)
            rank = jnp.zeros((blk, LANES), jnp.int32)
            for jp in range(c):
                gj = gate[:, GATE_LANE0 + jp:GATE_LANE0 + jp + 1]
                beats = (gj > gate) | ((gj == gate) & (blockid > jp))
                rank = rank + beats.astype(jnp.int32)
            dropped = valid & (rank >= MOBA_TOPK)
        else:
            dropped = None
        extra = jnp.where(col < GATE_LANE0, slope, 0.0)
        if dropped is not None:
            extra = jnp.where(dropped, NEG_INF, extra)
        extra = jnp.concatenate([zeros, extra] if hh < HEADS_PER_STEP // 2 else [extra, zeros], axis=1)
        qa = jnp.where(mine, q * ATTN_HEAD_DIM ** -0.5, extra).astype(BF16)
        var = 0 if hh < HEADS_PER_STEP // 2 else 1
        tiles = [_dot(qa, kt_scr[var, j]) for j in range(c)]
        tiles.append(jnp.where(causal, _dot(qa, kt_scr[var, c]), NEG_INF))
        mx = tiles[0]
        for t in tiles[1:]:
            mx = jnp.maximum(mx, t)
        m = jnp.max(mx, axis=-1, keepdims=True)
        probs = [jnp.exp(t - m) for t in tiles]
        tot = probs[0]
        for p in probs[1:]:
            tot = tot + p
        l = jnp.sum(tot, axis=-1, keepdims=True)
        pb = jnp.concatenate([p.astype(BF16) for p in probs], axis=1)
        ctx = _dot(pb, vb_scr[0:(c + 1) * blk, :])
        out = jnp.where(mine, ctx / l, out)
    o_ref[...] = out


def _moba_kernel(slopes_ref, q_ref, k_ref, v_ref, o_ref, kt_scr, vb_scr, kmh_scr, kml_scr):
    hg = pl.program_id(1)
    i = pl.program_id(2)
    nb = vb_scr.shape[0] // MOBA_BLOCK

    @pl.when(i == 0)
    def _():
        _moba_prepare(k_ref, v_ref, kt_scr, vb_scr, kmh_scr, kml_scr, nb)

    for c in range(nb):
        @pl.when(i == c)
        def _():
            _moba_block(c, hg, slopes_ref, q_ref, o_ref, kt_scr, vb_scr, kmh_scr, kml_scr, nb)


def _moba_call(q, k, v, bsz, seq):
    nb = seq // MOBA_BLOCK
    width = HEADS_PER_STEP * ATTN_HEAD_DIM
    slopes = jnp.asarray([2.0 ** (-8.0 * (h + 1) / ATTN_HEADS) for h in range(ATTN_HEADS)], F32)
    qspec = pl.BlockSpec((MOBA_BLOCK, width), lambda b, g, i: (b * nb + i, g))
    kvspec = pl.BlockSpec((seq, width), lambda b, g, i: (b, g))
    return pl.pallas_call(
        _moba_kernel, grid=(bsz, D_ATTN // width, nb),
        in_specs=[pl.BlockSpec(memory_space=pltpu.SMEM), qspec, kvspec, kvspec],
        out_specs=qspec, out_shape=jax.ShapeDtypeStruct(q.shape, F32),
        scratch_shapes=[pltpu.VMEM((2, nb, width, MOBA_BLOCK), BF16),
                        pltpu.VMEM((seq, width), BF16),
                        pltpu.VMEM((LANES, width), BF16),
                        pltpu.VMEM((LANES, width), BF16)],
        compiler_params=_params("parallel", "parallel", "arbitrary"), name="moba")(slopes, q, k, v)


def _s5_kernel(u_ref, bmat_ref, apow_ref, cmat_ref, d_ref, wglu_ref, bglu_ref, o_ref,
               half_scr, up_scr, x_scr, xb_scr, carry_scr):
    c = pl.program_id(1)
    chunk = u_ref.shape[0]
    steps = chunk // SUBLANES
    ns = D_STATE

    @pl.when(c == 0)
    def _():
        carry_scr[...] = jnp.zeros_like(carry_scr)

    halves = [slice(k * LANES, (k + 1) * LANES) for k in range(u_ref.shape[1] // LANES)]
    for k, cols in enumerate(halves):
        half_scr[k] = u_ref[:, cols]
        for tau in range(steps):
            up_scr[tau * SUBLANES:(tau + 1) * SUBLANES, cols] = (
                half_scr[k, pl.ds(tau, SUBLANES, stride=steps), :])
    up = up_scr[...]
    x_scr[...] = _dot(up.astype(BF16), bmat_ref[...])

    a1r = jnp.broadcast_to(apow_ref[0:1, 0:ns], (SUBLANES, ns))
    a1i = jnp.broadcast_to(apow_ref[0:1, ns:2 * ns], (SUBLANES, ns))

    def local_step(tau, state):
        xr, xi = state
        rows = pl.ds(pl.multiple_of(tau * SUBLANES, SUBLANES), SUBLANES)
        xr, xi = (a1r * xr - a1i * xi + x_scr[rows, 0:ns],
                  a1r * xi + a1i * xr + x_scr[rows, ns:2 * ns])
        x_scr[rows, 0:ns] = xr
        x_scr[rows, ns:2 * ns] = xi
        return xr, xi

    zero = jnp.zeros((SUBLANES, ns), F32)
    end_r, end_i = lax.fori_loop(0, steps, local_step, (zero, zero), unroll=4)

    alr = apow_ref[steps - 1:steps, 0:ns]
    ali = apow_ref[steps - 1:steps, ns:2 * ns]
    er = carry_scr[:, 0:ns]
    ei = carry_scr[:, ns:2 * ns]
    enter_r, enter_i = [], []
    for s in range(SUBLANES):
        enter_r.append(er)
        enter_i.append(ei)
        er, ei = (end_r[s:s + 1, :] + alr * er - ali * ei,
                  end_i[s:s + 1, :] + alr * ei + ali * er)
    carry_scr[:, 0:ns] = er
    carry_scr[:, ns:2 * ns] = ei
    cr = jnp.concatenate(enter_r, axis=0)
    ci = jnp.concatenate(enter_i, axis=0)

    def carry_step(pair, _):
        halves_r, halves_i = [], []
        for k in range(2):
            tau = 2 * pair + k
            rows = pl.ds(pl.multiple_of(tau * SUBLANES, SUBLANES), SUBLANES)
            pr = jnp.broadcast_to(apow_ref[pl.ds(tau, 1), 0:ns], (SUBLANES, ns))
            pi = jnp.broadcast_to(apow_ref[pl.ds(tau, 1), ns:2 * ns], (SUBLANES, ns))
            halves_r.append(x_scr[rows, 0:ns] + pr * cr - pi * ci)
            halves_i.append(x_scr[rows, ns:2 * ns] + pr * ci + pi * cr)
        rows2 = pl.ds(pl.multiple_of(pair * 2 * SUBLANES, 2 * SUBLANES), 2 * SUBLANES)
        xb_scr[rows2, 0:ns] = jnp.concatenate(halves_r, axis=0).astype(BF16)
        xb_scr[rows2, ns:2 * ns] = jnp.concatenate(halves_i, axis=0).astype(BF16)
        return 0

    lax.fori_loop(0, steps // 2, carry_step, 0, unroll=2)

    y = _dot(xb_scr[...], cmat_ref[...])
    y = jax.nn.gelu(y + d_ref[...] * up)
    gate = jax.nn.sigmoid(_dot(y.astype(BF16), wglu_ref[...]) + bglu_ref[...])
    up_scr[...] = y * gate
    for k, cols in enumerate(halves):
        for tau in range(steps):
            half_scr[k, pl.ds(tau, SUBLANES, stride=steps), :] = (
                up_scr[tau * SUBLANES:(tau + 1) * SUBLANES, cols])
        o_ref[:, cols] = half_scr[k]


def _complex_mul(ar, ai, br, bi):
    return ar * br - ai * bi, ar * bi + ai * br


def _s5_tables(a_re, a_im, b_re, b_im, c_re, c_im, log_dt):
    g, p, h = b_re.shape
    dt = jnp.exp(log_dt)[:, None]
    mag = jnp.exp(dt * a_re)
    ab_re, ab_im = mag * jnp.cos(dt * a_im), mag * jnp.sin(dt * a_im)
    den = a_re * a_re + a_im * a_im
    nr, ni = ab_re - 1.0, ab_im
    f_re = (nr * a_re + ni * a_im) / den
    f_im = (ni * a_re - nr * a_im) / den
    bb_re = f_re[..., None] * b_re - f_im[..., None] * b_im
    bb_im = f_re[..., None] * b_im + f_im[..., None] * b_re
    eye = jnp.eye(g, dtype=F32)
    bmat_re = jnp.einsum('gph,gk->ghkp', bb_re, eye).reshape(g * h, g * p)
    bmat_im = jnp.einsum('gph,gk->ghkp', bb_im, eye).reshape(g * h, g * p)
    bmat = jnp.concatenate([bmat_re, bmat_im], axis=1).astype(BF16)
    cmat_re = jnp.einsum('ghp,gk->gpkh', c_re, eye).reshape(g * p, g * h)
    cmat_im = jnp.einsum('ghp,gk->gpkh', c_im, eye).reshape(g * p, g * h)
    cmat = jnp.concatenate([cmat_re, -cmat_im], axis=0).astype(BF16)
    pr, pi = ab_re.reshape(1, g * p), ab_im.reshape(1, g * p)
    steps = SCAN_CHUNK // SUBLANES
    while pr.shape[0] < steps:
        nr2, ni2 = _complex_mul(pr, pi, pr[-1:], pi[-1:])
        pr, pi = jnp.concatenate([pr, nr2], axis=0), jnp.concatenate([pi, ni2], axis=0)
    apow = jnp.concatenate([pr[:steps], pi[:steps]], axis=1)
    return bmat, apow, cmat


def _s5_call(u, tables, d_skip, w_glu, b_glu, bsz, seq):
    bmat, apow, cmat = tables
    n, d = u.shape
    nc = seq // SCAN_CHUNK
    row = pl.BlockSpec((SCAN_CHUNK, d), lambda b, c: (b * nc + c, 0))
    return pl.pallas_call(
        _s5_kernel, grid=(bsz, nc),
        in_specs=[row, _const_spec(bmat.shape), _const_spec(apow.shape),
                  _const_spec(cmat.shape), _const_spec((1, d)), _const_spec((d, d)),
                  _const_spec((1, d))],
        out_specs=row, out_shape=jax.ShapeDtypeStruct((n, d), F32),
        scratch_shapes=[pltpu.VMEM((d // LANES, SCAN_CHUNK, LANES), F32),
                        pltpu.VMEM((SCAN_CHUNK, d), F32),
                        pltpu.VMEM((SCAN_CHUNK, 2 * D_STATE), F32),
                        pltpu.VMEM((SCAN_CHUNK, 2 * D_STATE), BF16),
                        pltpu.VMEM((1, 2 * D_STATE), F32)],
        compiler_params=_params("parallel", "arbitrary"), name="s5")(
            u, bmat, apow, cmat, d_skip.reshape(1, d), w_glu.astype(BF16), b_glu.reshape(1, d))


def _lru_kernel(xl_ref, gl_ref, cw_ref, cb_ref, wa_ref, ba_ref, wx_ref, bx_ref, lam_ref, o_ref,
                ext_scr, a_scr, h_scr, carry_scr):
    c = pl.program_id(1)
    chunk = xl_ref.shape[0]
    halo = SUBLANES

    @pl.when(c == 0)
    def _():
        ext_scr[0:halo, :] = jnp.zeros((halo, ext_scr.shape[1]), F32)
        carry_scr[...] = jnp.zeros_like(carry_scr)

    xl = xl_ref[...]
    ext_scr[halo:, :] = xl
    ext = ext_scr[...]
    xc = cb_ref[...] + cw_ref[CONV_WIDTH - 1:CONV_WIDTH, :] * xl
    for back in range(1, CONV_WIDTH):
        shifted = pltpu.roll(ext, back, 0)[halo:, :]
        xc = xc + cw_ref[CONV_WIDTH - 1 - back:CONV_WIDTH - back, :] * shifted
    ext_scr[0:halo, :] = xl[chunk - halo:, :]

    xb = xc.astype(BF16)
    r = jax.nn.sigmoid(_dot(xb, wa_ref[...]) + ba_ref[...])
    gate_in = jax.nn.sigmoid(_dot(xb, wx_ref[...]) + bx_ref[...])
    neg_lam = -lam_ref[...]
    softplus = jnp.maximum(neg_lam, 0.0) + jnp.log1p(jnp.exp(-jnp.abs(neg_lam)))
    log_a = -LRU_C * r * softplus
    a = jnp.exp(log_a)
    one_minus_a2 = jnp.tanh(-log_a) * (1.0 + a * a)
    a_scr[...] = a
    h_scr[...] = jnp.sqrt(one_minus_a2) * (gate_in * xc)

    sub = lax.broadcasted_iota(jnp.int32, (SUBLANES, a.shape[1]), 0)

    def tile_body(t, carry):
        rows = pl.ds(pl.multiple_of(t * SUBLANES, SUBLANES), SUBLANES)
        at = a_scr[rows, :]
        bt = h_scr[rows, :]
        for lvl in range(3):
            keep = sub >= (1 << lvl)
            a_sh = jnp.where(keep, pltpu.roll(at, 1 << lvl, 0), 1.0)
            b_sh = jnp.where(keep, pltpu.roll(bt, 1 << lvl, 0), 0.0)
            bt = bt + at * b_sh
            at = at * a_sh
        ht = bt + at * carry
        h_scr[rows, :] = ht
        return ht[SUBLANES - 1:SUBLANES, :]

    carry_scr[...] = lax.fori_loop(0, chunk // SUBLANES, tile_body, carry_scr[...], unroll=2)
    o_ref[...] = h_scr[...] * jax.nn.gelu(gl_ref[...])


def _block_diag(w):
    heads, di, do = w.shape
    eye = jnp.eye(heads, dtype=w.dtype)
    return jnp.einsum('hij,hk->hikj', w, eye).reshape(heads * di, heads * do)


def _lru_call(xl, gl, conv_w, conv_b, w_a, b_a, w_x, b_x, lam, bsz, seq):
    n, d = xl.shape
    nc = seq // SCAN_CHUNK
    row = pl.BlockSpec((SCAN_CHUNK, d), lambda b, c: (b * nc + c, 0))
    vec = _const_spec((1, d))
    mat = _const_spec((d, d))
    return pl.pallas_call(
        _lru_kernel, grid=(bsz, nc),
        in_specs=[row, row, _const_spec((CONV_WIDTH, d)), vec, mat, vec, mat, vec, vec],
        out_specs=row, out_shape=jax.ShapeDtypeStruct((n, d), F32),
        scratch_shapes=[pltpu.VMEM((SCAN_CHUNK + SUBLANES, d), F32),
                        pltpu.VMEM((SCAN_CHUNK, d), F32),
                        pltpu.VMEM((SCAN_CHUNK, d), F32),
                        pltpu.VMEM((1, d), F32)],
        compiler_params=_params("parallel", "arbitrary"), name="rglru")(
            xl, gl, conv_w, conv_b.reshape(1, d), _block_diag(w_a).astype(BF16), b_a.reshape(1, d),
            _block_diag(w_x).astype(BF16), b_x.reshape(1, d), lam.reshape(1, d))


def _out_proj_kernel(ya_ref, ys_ref, yl_ref, h_ref, mg_ref, w_ref, g_ref, b_ref, o_ref):
    mix = None
    start = 0
    for y_ref in (ya_ref, ys_ref, yl_ref):
        width = y_ref.shape[1]
        y = (_rms(y_ref[...]) * mg_ref[:, start:start + width]).astype(BF16)
        part = _dot(y, w_ref[start:start + width, :])
        mix = part if mix is None else mix + part
        start += width
    o_ref[...] = _layer_norm(ALPHA * h_ref[...] + mix, g_ref[...], b_ref[...])


def _out_proj_call(ya, ys, yl, h, mix_g, w_out, g, b):
    n, d = h.shape
    row = lambda w: pl.BlockSpec((ROW_TILE, w), lambda i: (i, 0))
    vec = _const_spec((1, d))
    return pl.pallas_call(
        _out_proj_kernel, grid=(n // ROW_TILE,),
        in_specs=[row(ya.shape[1]), row(ys.shape[1]), row(yl.shape[1]), row(d), vec,
                  _const_spec(w_out.shape), vec, vec],
        out_specs=row(d), out_shape=jax.ShapeDtypeStruct((n, d), F32),
        compiler_params=_params("parallel"), name="out_proj")(
            ya, ys, yl, h, mix_g.reshape(1, d), w_out, g.reshape(1, d), b.reshape(1, d))


def _mem_kv_kernel(mem_ref, wk_ref, wv_ref, kt_ref, v_ref):
    mb = mem_ref[...].astype(BF16)
    kt_ref[0] = _dot(mb, wk_ref[...]).T.astype(BF16)
    v_ref[0] = _dot(mb, wv_ref[...]).astype(BF16)


def _mem_kv_call(mem, wk, wv):
    bsz, m, d = mem.shape
    return pl.pallas_call(
        _mem_kv_kernel, grid=(bsz,),
        in_specs=[pl.BlockSpec((m, d), lambda b: (b, 0)), _const_spec((d, d)), _const_spec((d, d))],
        out_specs=[pl.BlockSpec((1, d, m), lambda b: (b, 0, 0)),
                   pl.BlockSpec((1, m, d), lambda b: (b, 0, 0))],
        out_shape=[jax.ShapeDtypeStruct((bsz, d, m), BF16), jax.ShapeDtypeStruct((bsz, m, d), BF16)],
        compiler_params=_params("parallel"), name="mem_kv")(mem.reshape(bsz * m, d), wk, wv)


def _cross_kernel(h_ref, kt_ref, v_ref, wq_ref, wo_ref, g_ref, b_ref, wr_ref, br_ref,
                  haug_ref, route_ref, count_ref, ctx_scr, run_scr):
    @pl.when(pl.program_id(0) == 0)
    def _():
        run_scr[...] = jnp.zeros_like(run_scr)

    h = h_ref[...]
    d = h.shape[1]
    hd = d // MEM_HEADS
    q = _dot(h.astype(BF16), wq_ref[...])
    scale = hd ** -0.5
    for hh in range(MEM_HEADS):
        cols = slice(hh * hd, (hh + 1) * hd)
        s = _dot(q[:, cols].astype(BF16), kt_ref[0, cols, :]) * scale
        s = s - jnp.max(s, axis=-1, keepdims=True)
        p = jnp.exp(s)
        p = p / jnp.sum(p, axis=-1, keepdims=True)
        ctx_scr[:, cols] = _dot(p.astype(BF16), v_ref[0, :, cols]).astype(BF16)
    cross = _dot(ctx_scr[...], wo_ref[...])
    h2 = _layer_norm(ALPHA * h + cross, g_ref[...], b_ref[...])

    rows = h2.shape[0]
    comb, g_idx = _route(h2, wr_ref, br_ref)
    lane = lax.broadcasted_iota(jnp.int32, (rows, LANES), 1)
    in_my_group = lane == g_idx
    onehot = jnp.where(in_my_group, 1.0, 0.0).astype(BF16)
    tri = jnp.where(lax.broadcasted_iota(jnp.int32, (rows, rows), 0)
                    >= lax.broadcasted_iota(jnp.int32, (rows, rows), 1), 1.0, 0.0).astype(BF16)
    seen = _dot(tri, onehot) + run_scr[...]
    rank = jnp.sum(jnp.where(in_my_group, seen - 1.0, 0.0), axis=-1, keepdims=True)
    run_scr[...] = seen[rows - 1:rows, :]
    route = jnp.where(lane == ROUTE_GROUP_LANE, g_idx.astype(F32),
                      jnp.where(lane == ROUTE_RANK_LANE, rank, comb))
    haug_ref[:, 0:d] = h2
    haug_ref[:, d:d + LANES] = route
    route_ref[...] = route
    count_ref[...] = jnp.broadcast_to(seen[rows - 1:rows, :], count_ref.shape)


def _cross_call(h, kt, v, wq, wo, g, b, wr, br, seq):
    n, d = h.shape
    m = v.shape[1]
    tiles_per_seq = seq // ROW_TILE
    row = lambda w: pl.BlockSpec((ROW_TILE, w), lambda i: (i, 0))
    vec = _const_spec((1, d))
    return pl.pallas_call(
        _cross_kernel, grid=(n // ROW_TILE,),
        in_specs=[row(d), pl.BlockSpec((1, d, m), lambda i: (i // tiles_per_seq, 0, 0)),
                  pl.BlockSpec((1, m, d), lambda i: (i // tiles_per_seq, 0, 0)),
                  _const_spec((d, d)), _const_spec((d, d)), vec, vec,
                  _const_spec(wr.shape), _const_spec(br.shape)],
        out_specs=[row(d + LANES), row(LANES), _const_spec((SUBLANES, LANES))],
        out_shape=[jax.ShapeDtypeStruct((n, d + LANES), F32), jax.ShapeDtypeStruct((n, LANES), F32),
                   jax.ShapeDtypeStruct((SUBLANES, LANES), F32)],
        scratch_shapes=[pltpu.VMEM((ROW_TILE, d), BF16), pltpu.VMEM((1, LANES), F32)],
        compiler_params=_params("arbitrary"), name="cross_attn")(
            h, kt, v, wq, wo, g.reshape(1, d), b.reshape(1, d), wr, br)


def _first_max(x, valid, lane):
    xm = jnp.where(valid, x, -jnp.inf)
    mx = jnp.max(xm, axis=-1, keepdims=True)
    idx = jnp.min(jnp.where(valid & (xm == mx), lane, LANES), axis=-1, keepdims=True)
    return xm, mx, idx


def _route(h, wr_ref, br_ref):
    rows = h.shape[0]
    logits = _dot3(h, wr_ref[...], _dot) + br_ref[...]
    lane = lax.broadcasted_iota(jnp.int32, (rows, LANES), 1)
    g_logits, g_max, g_idx = _first_max(logits, lane < N_GROUPS, lane)
    g_w = 1.0 / jnp.sum(jnp.exp(g_logits - g_max), axis=-1, keepdims=True)
    expert = lane - N_GROUPS
    group_of = lax.shift_right_arithmetic(expert, int(math.log2(EXPERTS_PER_GROUP)))
    in_group = (expert >= 0) & (expert < N_EXPERTS) & (group_of == g_idx)
    e_logits, e_max, _ = _first_max(logits, in_group, lane)
    e_exp = jnp.exp(e_logits - e_max)
    e_prob = e_exp / jnp.sum(e_exp, axis=-1, keepdims=True)
    _, p1, i1 = _first_max(e_prob, in_group, lane)
    _, p2, i2 = _first_max(e_prob, in_group & (lane != i1), lane)
    denom = p1 + p2
    local = jnp.where(lane == i1, p1 / denom, jnp.where(lane == i2, p2 / denom, 0.0))
    return local * g_w, g_idx


def _router_tables(wr_g, br_g, wr_e, br_e):
    d = wr_g.shape[0]
    pad = LANES - N_GROUPS - N_EXPERTS
    wr = jnp.concatenate([wr_g, wr_e, jnp.zeros((d, pad), F32)], axis=1)
    br = jnp.concatenate([br_g, br_e, jnp.zeros((pad,), F32)]).reshape(1, LANES)
    return wr, br


def _permute_kernel(scatter, pos_ref, src_ref, dst_ref, sem):
    base = pl.program_id(0) * PERM_ROWS

    def row_copy(r):
        there = pl.ds(pos_ref[base + r], 1)
        here = pl.ds(r, 1)
        if scatter:
            return pltpu.make_async_copy(src_ref.at[here, :], dst_ref.at[there, :], sem)
        return pltpu.make_async_copy(src_ref.at[there, :], dst_ref.at[here, :], sem)

    def start(i, carry):
        for k in range(PERM_UNROLL):
            row_copy(i * PERM_UNROLL + k).start(priority=k % 2)
        return carry

    def wait(i, carry):
        for k in range(PERM_UNROLL):
            row_copy(i * PERM_UNROLL + k).wait()
        return carry

    lax.fori_loop(0, PERM_ROWS // PERM_UNROLL, start, 0)
    lax.fori_loop(0, PERM_ROWS // PERM_UNROLL, wait, 0)


def _permute_call(src, pos, scatter, name):
    n, w = src.shape
    block = pl.BlockSpec((PERM_ROWS, w), lambda i, pos: (i, 0))
    hbm = pl.BlockSpec(memory_space=pl.ANY)
    grid_spec = pltpu.PrefetchScalarGridSpec(
        num_scalar_prefetch=1, grid=(n // PERM_ROWS,),
        in_specs=[block if scatter else hbm], out_specs=hbm if scatter else block,
        scratch_shapes=[pltpu.SemaphoreType.DMA(())])
    return pl.pallas_call(
        functools.partial(_permute_kernel, scatter), grid_spec=grid_spec,
        out_shape=jax.ShapeDtypeStruct((n, w), src.dtype),
        compiler_params=_params("arbitrary"), name=name)(pos, src)


def _moe_kernel(tile_ref, group_ref, flag_ref, x_ref, wg_ref, wu_ref, wd_ref, g_ref, b_ref, o_ref,
                hb_scr, acc_scr):
    p = pl.program_id(0)
    e = pl.program_id(1)
    d = o_ref.shape[1]
    flags = flag_ref[p]

    @pl.when(((flags & PAIR_FIRST) != 0) & (e == 0))
    def _():
        hb_scr[...] = x_ref[:, 0:d].astype(BF16)
        acc_scr[...] = jnp.zeros_like(acc_scr)

    @pl.when((flags & PAIR_VALID) != 0)
    def _():
        hb = hb_scr[...]
        he = jax.nn.silu(_dot(hb, wg_ref[0])) * _dot(hb, wu_ref[0])
        y = _dot(he.astype(BF16), wd_ref[0])
        route = x_ref[:, d:d + LANES]
        lane = lax.broadcasted_iota(jnp.int32, route.shape, 1)
        mine = lane == N_GROUPS + group_ref[p] * EXPERTS_PER_GROUP + e
        weight = jnp.sum(jnp.where(mine, route, 0.0), axis=-1, keepdims=True)
        acc_scr[...] += weight * y

    @pl.when(((flags & PAIR_LAST) != 0) & (e == pl.num_programs(1) - 1))
    def _():
        o_ref[...] = _layer_norm(ALPHA * x_ref[:, 0:d] + acc_scr[...], g_ref[...], b_ref[...])


def _moe_call(xs, pair_tile, pair_group, pair_flags, w_gate, w_up, w_down, g, b):
    n, daug = xs.shape
    d = daug - LANES
    _, _, de = w_gate.shape
    last_e = EXPERTS_PER_GROUP - 1

    def expert(p, e, tile, group, flags):
        return group[p] * EXPERTS_PER_GROUP + jnp.where((flags[p] & PAIR_VALID) != 0, e, last_e)

    vec = pl.BlockSpec((1, d), lambda p, e, tile, group, flags: (0, 0))
    grid_spec = pltpu.PrefetchScalarGridSpec(
        num_scalar_prefetch=3, grid=(pair_tile.shape[0], EXPERTS_PER_GROUP),
        in_specs=[pl.BlockSpec((MOE_TILE, daug), lambda p, e, tile, group, flags: (tile[p], 0)),
                  pl.BlockSpec((1, d, de), lambda p, e, *s: (expert(p, e, *s), 0, 0)),
                  pl.BlockSpec((1, d, de), lambda p, e, *s: (expert(p, e, *s), 0, 0)),
                  pl.BlockSpec((1, de, d), lambda p, e, *s: (expert(p, e, *s), 0, 0)), vec, vec],
        out_specs=pl.BlockSpec((MOE_TILE, d), lambda p, e, tile, group, flags: (tile[p], 0)),
        scratch_shapes=[pltpu.VMEM((MOE_TILE, d), BF16), pltpu.VMEM((MOE_TILE, d), F32)])
    return pl.pallas_call(
        _moe_kernel, grid_spec=grid_spec, out_shape=jax.ShapeDtypeStruct((n, d), F32),
        compiler_params=_params("arbitrary", "arbitrary"), name="moe")(
            pair_tile, pair_group, pair_flags, xs, w_gate, w_up, w_down, g.reshape(1, d), b.reshape(1, d))


def _moe_tables(route, counts, n):
    gid = route[:, ROUTE_GROUP_LANE].astype(jnp.int32)
    rank = route[:, ROUTE_RANK_LANE].astype(jnp.int32)
    cnt = counts[0, :N_GROUPS].astype(jnp.int32)
    ends = jnp.cumsum(cnt)
    starts = ends - cnt
    pos = starts[gid] + rank
    tiles = n // MOE_TILE
    pairs = tiles + N_GROUPS - 1
    lo = jnp.arange(tiles, dtype=jnp.int32)[:, None] * MOE_TILE
    active = ((starts[None, :] < lo + MOE_TILE) & (ends[None, :] > lo)).reshape(-1)
    n_active = jnp.sum(active.astype(jnp.int32))
    idx = jnp.nonzero(active, size=pairs, fill_value=0)[0].astype(jnp.int32)
    valid = jnp.arange(pairs, dtype=jnp.int32) < n_active
    idx = jnp.where(valid, idx, idx[n_active - 1])
    tile = idx // N_GROUPS
    group = idx % N_GROUPS
    prev_tile = jnp.concatenate([jnp.full((1,), -1, jnp.int32), tile[:-1]])
    next_tile = jnp.concatenate([tile[1:], jnp.full((1,), -1, jnp.int32)])
    next_valid = jnp.concatenate([valid[1:], jnp.zeros((1,), bool)])
    first = valid & (tile != prev_tile)
    last = valid & ((tile != next_tile) | ~next_valid)
    flags = (valid.astype(jnp.int32) * PAIR_VALID + first.astype(jnp.int32) * PAIR_FIRST
             + last.astype(jnp.int32) * PAIR_LAST)
    return pos, tile, group, flags


def _moe_full(haug, route, counts, w_gate, w_up, w_down, g, b):
    n = haug.shape[0]
    pos, tile, group, flags = _moe_tables(route, counts, n)
    xs = _permute_call(haug, pos, True, "moe_dispatch")
    ys = _moe_call(xs, tile, group, flags, w_gate.astype(BF16), w_up.astype(BF16),
                   w_down.astype(BF16), g, b)
    return _permute_call(ys, pos, False, "moe_return")


def kernel(x, mem, ln0_g, ln0_b, w_in, mix_g, w_out, ssm_a_re, ssm_a_im, ssm_b_re, ssm_b_im, ssm_c_re, ssm_c_im, ssm_d, ssm_log_dt, ssm_w_glu, ssm_b_glu, lru_conv_w, lru_conv_b, lru_w_a, lru_b_a, lru_w_x, lru_b_x, lru_lam, ln1_g, ln1_b, mem_wq, mem_wk, mem_wv, mem_wo, ln2_g, ln2_b, moe_wr_g, moe_br_g, moe_wr_e, moe_br_e, moe_w_gate, moe_w_up, moe_w_down, ln3_g, ln3_b):
    bsz, seq, d = x.shape
    depth = w_in.shape[0]
    assert seq % MOBA_BLOCK == 0 and seq % SCAN_CHUNK == 0 and seq % ROW_TILE == 0
    assert seq // MOBA_BLOCK <= SUBLANES
    h = x.reshape(bsz * seq, d)
    for l in range(depth):
        if l == 0:
            h, q, k, v, u, xl, gl = _in_proj_call(h, w_in[l].astype(BF16), ln=(ln0_g, ln0_b))
        else:
            q, k, v, u, xl, gl = _in_proj_call(h, w_in[l].astype(BF16))
        y_attn = _moba_call(q, k, v, bsz, seq)
        tables = _s5_tables(ssm_a_re[l], ssm_a_im[l], ssm_b_re[l], ssm_b_im[l], ssm_c_re[l],
                            ssm_c_im[l], ssm_log_dt[l])
        y_ssm = _s5_call(u, tables, ssm_d[l], ssm_w_glu[l], ssm_b_glu[l], bsz, seq)
        y_lru = _lru_call(xl, gl, lru_conv_w[l], lru_conv_b[l], lru_w_a[l], lru_b_a[l],
                          lru_w_x[l], lru_b_x[l], lru_lam[l], bsz, seq)
        h = _out_proj_call(y_attn, y_ssm, y_lru, h, mix_g[l], w_out[l].astype(BF16),
                           ln1_g[l], ln1_b[l])
        kt, vm = _mem_kv_call(mem, mem_wk[l].astype(BF16), mem_wv[l].astype(BF16))
        wr, br = _router_tables(moe_wr_g[l], moe_br_g[l], moe_wr_e[l], moe_br_e[l])
        haug, route, counts = _cross_call(h, kt, vm, mem_wq[l].astype(BF16), mem_wo[l].astype(BF16),
                                          ln2_g[l], ln2_b[l], wr, br, seq)
        h = _moe_full(haug, route, counts, moe_w_gate[l], moe_w_up[l], moe_w_down[l],
                      ln3_g[l], ln3_b[l])
    return h.reshape(bsz, seq, d)
```

```python
import functools
import math

import jax
import jax.numpy as jnp
from jax import lax
from jax.experimental import pallas as pl
from jax.experimental.pallas import tpu as pltpu

F32 = jnp.float32
BF16 = jnp.bfloat16

D_ATTN = 512
D_SSM = 256
D_LRU = 256
SSM_GROUPS = 16
SSM_GROUP = 16
SSM_STATE = 64
D_STATE = SSM_GROUPS * SSM_STATE
LRU_HEADS = 4
CONV_WIDTH = 4
LRU_C = 8.0
ATTN_HEADS = 8
ATTN_HEAD_DIM = 64
HEADS_PER_STEP = 4
MOBA_BLOCK = 256
MOBA_TOPK = 3
MEM_HEADS = 4
N_GROUPS = 4
EXPERTS_PER_GROUP = 4
N_EXPERTS = 16
DEPTH = 2
ALPHA = (2.0 * DEPTH) ** 0.25
LN_EPS = 1e-5
RMS_EPS = 1e-6
NEG_INF = -1e30

SUBLANES = 8
LANES = 128
VMEM_LIMIT = 48 * 1024 * 1024
ROW_TILE = 512
MOE_TILE = 1024
PERM_ROWS = 1024
PERM_UNROLL = 16
ROUTE_GROUP_LANE = 0
ROUTE_RANK_LANE = 1
PAIR_VALID, PAIR_FIRST, PAIR_LAST = 1, 2, 4
SCAN_CHUNK = 512


def _params(*sem):
    return pltpu.CompilerParams(dimension_semantics=sem, vmem_limit_bytes=VMEM_LIMIT)


def _const_spec(shape):
    zeros = (0,) * len(shape)
    return pl.BlockSpec(shape, lambda *_: zeros)


def _layer_norm(x, g, b):
    mu = jnp.mean(x, axis=-1, keepdims=True)
    xc = x - mu
    var = jnp.mean(xc * xc, axis=-1, keepdims=True)
    return xc * lax.rsqrt(var + LN_EPS) * g + b


def _rms(x):
    return x * lax.rsqrt(jnp.mean(x * x, axis=-1, keepdims=True) + RMS_EPS)


def _dot(a, b):
    return jnp.dot(a, b, preferred_element_type=F32)


def _dot_nt(a, b):
    return lax.dot_general(a, b, (((1,), (1,)), ((), ())), preferred_element_type=F32)


def _split_bf16(x):
    hi = x.astype(BF16)
    lo = (x - hi.astype(F32)).astype(BF16)
    return hi, lo


def _dot3(a, b, dot):
    ah, al = _split_bf16(a)
    bh, bl = _split_bf16(b)
    return dot(ah, bh) + dot(ah, bl) + dot(al, bh)


_IN_SPLITS = (D_ATTN, D_ATTN, D_ATTN, D_SSM, D_LRU, D_LRU)


def _project(hb, w_ref, out_refs):
    start = 0
    for o_ref, width in zip(out_refs, _IN_SPLITS):
        o_ref[...] = _dot(hb, w_ref[:, start:start + width])
        start += width


def _in_proj_kernel(h_ref, w_ref, *out_refs):
    _project(h_ref[...].astype(BF16), w_ref, out_refs)


def _ln_in_proj_kernel(x_ref, g_ref, b_ref, w_ref, h_ref, *out_refs):
    h = _layer_norm(x_ref[...], g_ref[...], b_ref[...])
    h_ref[...] = h
    _project(h.astype(BF16), w_ref, out_refs)


def _in_proj_call(h, w_in, ln=None):
    n, d = h.shape
    row = lambda w: pl.BlockSpec((ROW_TILE, w), lambda i: (i, 0))
    widths = _IN_SPLITS if ln is None else (d,) + _IN_SPLITS
    vecs = [] if ln is None else [v.reshape(1, d) for v in ln]
    return pl.pallas_call(
        _in_proj_kernel if ln is None else _ln_in_proj_kernel, grid=(n // ROW_TILE,),
        in_specs=[row(d)] + [_const_spec((1, d))] * len(vecs) + [_const_spec(w_in.shape)],
        out_specs=[row(w) for w in widths],
        out_shape=[jax.ShapeDtypeStruct((n, w), F32) for w in widths],
        compiler_params=_params("parallel"), name="in_proj")(h, *vecs, w_in)


AUG_ROWS = 16
ALIBI_LANE0 = SUBLANES
GATE_STRIDE = 32


def _moba_prepare(k_ref, v_ref, kt_scr, vb_scr, kmh_scr, kml_scr, nb):
    blk = MOBA_BLOCK
    width = k_ref.shape[1]
    arow = lax.broadcasted_iota(jnp.int32, (AUG_ROWS, blk), 0)
    akey = lax.broadcasted_iota(jnp.int32, (AUG_ROWS, blk), 1).astype(F32)
    lane_head = lax.shift_right_logical(lax.broadcasted_iota(jnp.int32, (1, width), 1),
                                        int(math.log2(ATTN_HEAD_DIM)))
    gate_rows = []
    for j in range(nb):
        kj = k_ref[j * blk:(j + 1) * blk, :]
        ktj = kj.T.astype(BF16)
        aug = jnp.where(arow == ALIBI_LANE0, float(blk * j),
                        jnp.where(arow == ALIBI_LANE0 + 1, akey, jnp.where(arow == j, 1.0, 0.0)))
        aug = aug.astype(BF16)
        kt_scr[0, j] = ktj
        kt_scr[0, j, LANES:LANES + AUG_ROWS, :] = aug
        kt_scr[1, j] = ktj
        kt_scr[1, j, 0:AUG_ROWS, :] = aug
        vb_scr[j * blk:(j + 1) * blk, :] = v_ref[j * blk:(j + 1) * blk, :].astype(BF16)
        gate_rows.append(jnp.mean(kj, axis=0, keepdims=True))
    gate_rows.append(jnp.zeros((GATE_STRIDE - nb, width), F32))
    kmean = jnp.concatenate(gate_rows, axis=0)
    table = jnp.concatenate([jnp.where(lane_head == h, kmean, 0.0) for h in range(HEADS_PER_STEP)], axis=0)
    hi, lo = _split_bf16(table)
    kmh_scr[...] = hi
    kml_scr[...] = lo


def _moba_block(c, hg, slopes_ref, q_ref, o_ref, kt_scr, vb_scr, kmh_scr, kml_scr, nb):
    blk = MOBA_BLOCK
    width = q_ref.shape[1]
    q = q_ref[...]
    lane_head = lax.shift_right_logical(lax.broadcasted_iota(jnp.int32, (blk, width), 1),
                                        int(math.log2(ATTN_HEAD_DIM)))
    col = lax.broadcasted_iota(jnp.int32, (blk, LANES), 1)
    is_alibi = (col >= ALIBI_LANE0) & (col < ALIBI_LANE0 + 2)
    causal = (lax.broadcasted_iota(jnp.int32, (blk, blk), 0)
              >= lax.broadcasted_iota(jnp.int32, (blk, blk), 1))
    zeros = jnp.zeros((blk, LANES), F32)
    if c > MOBA_TOPK:
        qhi, qlo = _split_bf16(q)
        gate_t = (_dot_nt(kmh_scr[...], qhi) + _dot_nt(kml_scr[...], qhi)
                  + _dot_nt(kmh_scr[...], qlo))
        jrow = lax.broadcasted_iota(jnp.int32, (SUBLANES, blk), 0)
        pieces = []
        for hh in range(HEADS_PER_STEP):
            gate = jnp.where(jrow < c, gate_t[GATE_STRIDE * hh:GATE_STRIDE * hh + SUBLANES, :], NEG_INF)
            rank = jnp.zeros((SUBLANES, blk), jnp.int32)
            for jp in range(c):
                gj = jnp.broadcast_to(gate[jp:jp + 1, :], (SUBLANES, blk))
                beats = (gj > gate) | ((gj == gate) & (jrow > jp))
                rank = rank + beats.astype(jnp.int32)
            pieces.append(jnp.where((jrow < c) & (rank >= MOBA_TOPK), NEG_INF, 0.0))
            pieces.append(jnp.zeros((GATE_STRIDE - SUBLANES, blk), F32))
        penalty_all = jnp.concatenate(pieces, axis=0).T
    out = jnp.zeros((blk, width), F32)
    for hh in range(HEADS_PER_STEP):
        mine = lane_head == hh
        slope = slopes_ref[hg * HEADS_PER_STEP + hh]
        extra = jnp.where(is_alibi, slope, 0.0)
        if c > MOBA_TOPK:
            penalty = penalty_all if hh == 0 else pltpu.roll(penalty_all, LANES - GATE_STRIDE * hh, 1)
            extra = jnp.where(col < SUBLANES, penalty, extra)
        extra = jnp.concatenate([zeros, extra] if hh < HEADS_PER_STEP // 2 else [extra, zeros], axis=1)
        qa = jnp.where(mine, q * ATTN_HEAD_DIM ** -0.5, extra).astype(BF16)
        var = 0 if hh < HEADS_PER_STEP // 2 else 1
        tiles = [_dot(qa, kt_scr[var, j]) for j in range(c)]
        tiles.append(jnp.where(causal, _dot(qa, kt_scr[var, c]), NEG_INF))
        mx = tiles[0]
        for t in tiles[1:]:
            mx = jnp.maximum(mx, t)
        m = jnp.max(mx, axis=-1, keepdims=True)
        probs = [jnp.exp(t - m) for t in tiles]
        tot = probs[0]
        for p in probs[1:]:
            tot = tot + p
        l = jnp.sum(tot, axis=-1, keepdims=True)
        pb = jnp.concatenate([p.astype(BF16) for p in probs], axis=1)
        ctx = _dot(pb, vb_scr[0:(c + 1) * blk, :])
        out = jnp.where(mine, ctx / l, out)
    o_ref[...] = out


def _moba_kernel(slopes_ref, q_ref, k_ref, v_ref, o_ref, kt_scr, vb_scr, kmh_scr, kml_scr):
    hg = pl.program_id(1)
    i = pl.program_id(2)
    nb = vb_scr.shape[0] // MOBA_BLOCK

    @pl.when(i == 0)
    def _():
        _moba_prepare(k_ref, v_ref, kt_scr, vb_scr, kmh_scr, kml_scr, nb)

    for c in range(nb):
        @pl.when(i == c)
        def _():
            _moba_block(c, hg, slopes_ref, q_ref, o_ref, kt_scr, vb_scr, kmh_scr, kml_scr, nb)


def _moba_call(q, k, v, bsz, seq):
    nb = seq // MOBA_BLOCK
    width = HEADS_PER_STEP * ATTN_HEAD_DIM
    slopes = jnp.asarray([2.0 ** (-8.0 * (h + 1) / ATTN_HEADS) for h in range(ATTN_HEADS)], F32)
    qspec = pl.BlockSpec((MOBA_BLOCK, width), lambda b, g, i: (b * nb + i, g))
    kvspec = pl.BlockSpec((seq, width), lambda b, g, i: (b, g))
    return pl.pallas_call(
        _moba_kernel, grid=(bsz, D_ATTN // width, nb),
        in_specs=[pl.BlockSpec(memory_space=pltpu.SMEM), qspec, kvspec, kvspec],
        out_specs=qspec, out_shape=jax.ShapeDtypeStruct(q.shape, F32),
        scratch_shapes=[pltpu.VMEM((2, nb, width, MOBA_BLOCK), BF16),
                        pltpu.VMEM((seq, width), BF16),
                        pltpu.VMEM((LANES, width), BF16),
                        pltpu.VMEM((LANES, width), BF16)],
        compiler_params=_params("parallel", "parallel", "arbitrary"), name="moba")(slopes, q, k, v)


def _s5_kernel(u_ref, bmat_ref, apow_ref, cmat_ref, d_ref, wglu_ref, bglu_ref, o_ref,
               half_scr, up_scr, x_scr, xb_scr, carry_scr):
    c = pl.program_id(1)
    chunk = u_ref.shape[0]
    steps = chunk // SUBLANES
    ns = D_STATE

    @pl.when(c == 0)
    def _():
        carry_scr[...] = jnp.zeros_like(carry_scr)

    halves = [slice(k * LANES, (k + 1) * LANES) for k in range(u_ref.shape[1] // LANES)]
    for k, cols in enumerate(halves):
        half_scr[k] = u_ref[:, cols]
        for tau in range(steps):
            up_scr[tau * SUBLANES:(tau + 1) * SUBLANES, cols] = (
                half_scr[k, pl.ds(tau, SUBLANES, stride=steps), :])
    up = up_scr[...]
    x_scr[...] = _dot(up.astype(BF16), bmat_ref[...])

    a1r = jnp.broadcast_to(apow_ref[0:1, 0:ns], (SUBLANES, ns))
    a1i = jnp.broadcast_to(apow_ref[0:1, ns:2 * ns], (SUBLANES, ns))

    def local_step(tau, state):
        xr, xi = state
        rows = pl.ds(pl.multiple_of(tau * SUBLANES, SUBLANES), SUBLANES)
        xr, xi = (a1r * xr - a1i * xi + x_scr[rows, 0:ns],
                  a1r * xi + a1i * xr + x_scr[rows, ns:2 * ns])
        x_scr[rows, 0:ns] = xr
        x_scr[rows, ns:2 * ns] = xi
        return xr, xi

    zero = jnp.zeros((SUBLANES, ns), F32)
    end_r, end_i = lax.fori_loop(0, steps, local_step, (zero, zero), unroll=4)

    alr = apow_ref[steps - 1:steps, 0:ns]
    ali = apow_ref[steps - 1:steps, ns:2 * ns]
    er = carry_scr[:, 0:ns]
    ei = carry_scr[:, ns:2 * ns]
    enter_r, enter_i = [], []
    for s in range(SUBLANES):
        enter_r.append(er)
        enter_i.append(ei)
        er, ei = (end_r[s:s + 1, :] + alr * er - ali * ei,
                  end_i[s:s + 1, :] + alr * ei + ali * er)
    carry_scr[:, 0:ns] = er
    carry_scr[:, ns:2 * ns] = ei
    cr = jnp.concatenate(enter_r, axis=0)
    ci = jnp.concatenate(enter_i, axis=0)

    def carry_step(pair, _):
        halves_r, halves_i = [], []
        for k in range(2):
            tau = 2 * pair + k
            rows = pl.ds(pl.multiple_of(tau * SUBLANES, SUBLANES), SUBLANES)
            pr = jnp.broadcast_to(apow_ref[pl.ds(tau, 1), 0:ns], (SUBLANES, ns))
            pi = jnp.broadcast_to(apow_ref[pl.ds(tau, 1), ns:2 * ns], (SUBLANES, ns))
            halves_r.append(x_scr[rows, 0:ns] + pr * cr - pi * ci)
            halves_i.append(x_scr[rows, ns:2 * ns] + pr * ci + pi * cr)
        rows2 = pl.ds(pl.multiple_of(pair * 2 * SUBLANES, 2 * SUBLANES), 2 * SUBLANES)
        xb_scr[rows2, 0:ns] = jnp.concatenate(halves_r, axis=0).astype(BF16)
        xb_scr[rows2, ns:2 * ns] = jnp.concatenate(halves_i, axis=0).astype(BF16)
        return 0

    lax.fori_loop(0, steps // 2, carry_step, 0, unroll=2)

    y = _dot(xb_scr[...], cmat_ref[...])
    y = jax.nn.gelu(y + d_ref[...] * up)
    gate = jax.nn.sigmoid(_dot(y.astype(BF16), wglu_ref[...]) + bglu_ref[...])
    up_scr[...] = y * gate
    for k, cols in enumerate(halves):
        for tau in range(steps):
            half_scr[k, pl.ds(tau, SUBLANES, stride=steps), :] = (
                up_scr[tau * SUBLANES:(tau + 1) * SUBLANES, cols])
        o_ref[:, cols] = half_scr[k]


def _complex_mul(ar, ai, br, bi):
    return ar * br - ai * bi, ar * bi + ai * br


def _s5_tables(a_re, a_im, b_re, b_im, c_re, c_im, log_dt):
    g, p, h = b_re.shape
    dt = jnp.exp(log_dt)[:, None]
    mag = jnp.exp(dt * a_re)
    ab_re, ab_im = mag * jnp.cos(dt * a_im), mag * jnp.sin(dt * a_im)
    den = a_re * a_re + a_im * a_im
    nr, ni = ab_re - 1.0, ab_im
    f_re = (nr * a_re + ni * a_im) / den
    f_im = (ni * a_re - nr * a_im) / den
    bb_re = f_re[..., None] * b_re - f_im[..., None] * b_im
    bb_im = f_re[..., None] * b_im + f_im[..., None] * b_re
    eye = jnp.eye(g, dtype=F32)
    bmat_re = jnp.einsum('gph,gk->ghkp', bb_re, eye).reshape(g * h, g * p)
    bmat_im = jnp.einsum('gph,gk->ghkp', bb_im, eye).reshape(g * h, g * p)
    bmat = jnp.concatenate([bmat_re, bmat_im], axis=1).astype(BF16)
    cmat_re = jnp.einsum('ghp,gk->gpkh', c_re, eye).reshape(g * p, g * h)
    cmat_im = jnp.einsum('ghp,gk->gpkh', c_im, eye).reshape(g * p, g * h)
    cmat = jnp.concatenate([cmat_re, -cmat_im], axis=0).astype(BF16)
    pr, pi = ab_re.reshape(1, g * p), ab_im.reshape(1, g * p)
    steps = SCAN_CHUNK // SUBLANES
    while pr.shape[0] < steps:
        nr2, ni2 = _complex_mul(pr, pi, pr[-1:], pi[-1:])
        pr, pi = jnp.concatenate([pr, nr2], axis=0), jnp.concatenate([pi, ni2], axis=0)
    apow = jnp.concatenate([pr[:steps], pi[:steps]], axis=1)
    return bmat, apow, cmat


def _s5_call(u, tables, d_skip, w_glu, b_glu, bsz, seq):
    bmat, apow, cmat = tables
    n, d = u.shape
    nc = seq // SCAN_CHUNK
    row = pl.BlockSpec((SCAN_CHUNK, d), lambda b, c: (b * nc + c, 0))
    return pl.pallas_call(
        _s5_kernel, grid=(bsz, nc),
        in_specs=[row, _const_spec(bmat.shape), _const_spec(apow.shape),
                  _const_spec(cmat.shape), _const_spec((1, d)), _const_spec((d, d)),
                  _const_spec((1, d))],
        out_specs=row, out_shape=jax.ShapeDtypeStruct((n, d), F32),
        scratch_shapes=[pltpu.VMEM((d // LANES, SCAN_CHUNK, LANES), F32),
                        pltpu.VMEM((SCAN_CHUNK, d), F32),
                        pltpu.VMEM((SCAN_CHUNK, 2 * D_STATE), F32),
                        pltpu.VMEM((SCAN_CHUNK, 2 * D_STATE), BF16),
                        pltpu.VMEM((1, 2 * D_STATE), F32)],
        compiler_params=_params("parallel", "arbitrary"), name="s5")(
            u, bmat, apow, cmat, d_skip.reshape(1, d), w_glu.astype(BF16), b_glu.reshape(1, d))


def _lru_kernel(xl_ref, gl_ref, cw_ref, cb_ref, wa_ref, ba_ref, wx_ref, bx_ref, lam_ref, o_ref,
                ext_scr, a_scr, h_scr, carry_scr):
    c = pl.program_id(1)
    chunk = xl_ref.shape[0]
    halo = SUBLANES

    @pl.when(c == 0)
    def _():
        ext_scr[0:halo, :] = jnp.zeros((halo, ext_scr.shape[1]), F32)
        carry_scr[...] = jnp.zeros_like(carry_scr)

    xl = xl_ref[...]
    ext_scr[halo:, :] = xl
    ext = ext_scr[...]
    xc = cb_ref[...] + cw_ref[CONV_WIDTH - 1:CONV_WIDTH, :] * xl
    for back in range(1, CONV_WIDTH):
        shifted = pltpu.roll(ext, back, 0)[halo:, :]
        xc = xc + cw_ref[CONV_WIDTH - 1 - back:CONV_WIDTH - back, :] * shifted
    ext_scr[0:halo, :] = xl[chunk - halo:, :]

    xb = xc.astype(BF16)
    r = jax.nn.sigmoid(_dot(xb, wa_ref[...]) + ba_ref[...])
    gate_in = jax.nn.sigmoid(_dot(xb, wx_ref[...]) + bx_ref[...])
    neg_lam = -lam_ref[...]
    softplus = jnp.maximum(neg_lam, 0.0) + jnp.log1p(jnp.exp(-jnp.abs(neg_lam)))
    log_a = -LRU_C * r * softplus
    a = jnp.exp(log_a)
    one_minus_a2 = jnp.tanh(-log_a) * (1.0 + a * a)
    a_scr[...] = a
    h_scr[...] = jnp.sqrt(one_minus_a2) * (gate_in * xc)

    sub = lax.broadcasted_iota(jnp.int32, (SUBLANES, a.shape[1]), 0)

    def tile_body(t, carry):
        rows = pl.ds(pl.multiple_of(t * SUBLANES, SUBLANES), SUBLANES)
        at = a_scr[rows, :]
        bt = h_scr[rows, :]
        for lvl in range(3):
            keep = sub >= (1 << lvl)
            a_sh = jnp.where(keep, pltpu.roll(at, 1 << lvl, 0), 1.0)
            b_sh = jnp.where(keep, pltpu.roll(bt, 1 << lvl, 0), 0.0)
            bt = bt + at * b_sh
            at = at * a_sh
        ht = bt + at * carry
        h_scr[rows, :] = ht
        return ht[SUBLANES - 1:SUBLANES, :]

    carry_scr[...] = lax.fori_loop(0, chunk // SUBLANES, tile_body, carry_scr[...], unroll=2)
    o_ref[...] = h_scr[...] * jax.nn.gelu(gl_ref[...])


def _block_diag(w):
    heads, di, do = w.shape
    eye = jnp.eye(heads, dtype=w.dtype)
    return jnp.einsum('hij,hk->hikj', w, eye).reshape(heads * di, heads * do)


def _lru_call(xl, gl, conv_w, conv_b, w_a, b_a, w_x, b_x, lam, bsz, seq):
    n, d = xl.shape
    nc = seq // SCAN_CHUNK
    row = pl.BlockSpec((SCAN_CHUNK, d), lambda b, c: (b * nc + c, 0))
    vec = _const_spec((1, d))
    mat = _const_spec((d, d))
    return pl.pallas_call(
        _lru_kernel, grid=(bsz, nc),
        in_specs=[row, row, _const_spec((CONV_WIDTH, d)), vec, mat, vec, mat, vec, vec],
        out_specs=row, out_shape=jax.ShapeDtypeStruct((n, d), F32),
        scratch_shapes=[pltpu.VMEM((SCAN_CHUNK + SUBLANES, d), F32),
                        pltpu.VMEM((SCAN_CHUNK, d), F32),
                        pltpu.VMEM((SCAN_CHUNK, d), F32),
                        pltpu.VMEM((1, d), F32)],
        compiler_params=_params("parallel", "arbitrary"), name="rglru")(
            xl, gl, conv_w, conv_b.reshape(1, d), _block_diag(w_a).astype(BF16), b_a.reshape(1, d),
            _block_diag(w_x).astype(BF16), b_x.reshape(1, d), lam.reshape(1, d))


def _out_proj_kernel(ya_ref, ys_ref, yl_ref, h_ref, mg_ref, w_ref, g_ref, b_ref, o_ref):
    mix = None
    start = 0
    for y_ref in (ya_ref, ys_ref, yl_ref):
        width = y_ref.shape[1]
        y = (_rms(y_ref[...]) * mg_ref[:, start:start + width]).astype(BF16)
        part = _dot(y, w_ref[start:start + width, :])
        mix = part if mix is None else mix + part
        start += width
    o_ref[...] = _layer_norm(ALPHA * h_ref[...] + mix, g_ref[...], b_ref[...])


def _out_proj_call(ya, ys, yl, h, mix_g, w_out, g, b):
    n, d = h.shape
    row = lambda w: pl.BlockSpec((ROW_TILE, w), lambda i: (i, 0))
    vec = _const_spec((1, d))
    return pl.pallas_call(
        _out_proj_kernel, grid=(n // ROW_TILE,),
        in_specs=[row(ya.shape[1]), row(ys.shape[1]), row(yl.shape[1]), row(d), vec,
                  _const_spec(w_out.shape), vec, vec],
        out_specs=row(d), out_shape=jax.ShapeDtypeStruct((n, d), F32),
        compiler_params=_params("parallel"), name="out_proj")(
            ya, ys, yl, h, mix_g.reshape(1, d), w_out, g.reshape(1, d), b.reshape(1, d))


def _mem_kv_kernel(mem_ref, wk_ref, wv_ref, kt_ref, v_ref):
    mb = mem_ref[...].astype(BF16)
    kt_ref[0] = _dot(mb, wk_ref[...]).T.astype(BF16)
    v_ref[0] = _dot(mb, wv_ref[...]).astype(BF16)


def _mem_kv_call(mem, wk, wv):
    bsz, m, d = mem.shape
    return pl.pallas_call(
        _mem_kv_kernel, grid=(bsz,),
        in_specs=[pl.BlockSpec((m, d), lambda b: (b, 0)), _const_spec((d, d)), _const_spec((d, d))],
        out_specs=[pl.BlockSpec((1, d, m), lambda b: (b, 0, 0)),
                   pl.BlockSpec((1, m, d), lambda b: (b, 0, 0))],
        out_shape=[jax.ShapeDtypeStruct((bsz, d, m), BF16), jax.ShapeDtypeStruct((bsz, m, d), BF16)],
        compiler_params=_params("parallel"), name="mem_kv")(mem.reshape(bsz * m, d), wk, wv)


def _cross_kernel(h_ref, kt_ref, v_ref, wq_ref, wo_ref, g_ref, b_ref, wr_ref, br_ref,
                  haug_ref, route_ref, count_ref, ctx_scr, run_scr):
    @pl.when(pl.program_id(0) == 0)
    def _():
        run_scr[...] = jnp.zeros_like(run_scr)

    h = h_ref[...]
    d = h.shape[1]
    hd = d // MEM_HEADS
    q = _dot(h.astype(BF16), wq_ref[...])
    scale = hd ** -0.5
    for hh in range(MEM_HEADS):
        cols = slice(hh * hd, (hh + 1) * hd)
        s = _dot(q[:, cols].astype(BF16), kt_ref[0, cols, :]) * scale
        s = s - jnp.max(s, axis=-1, keepdims=True)
        p = jnp.exp(s)
        p = p / jnp.sum(p, axis=-1, keepdims=True)
        ctx_scr[:, cols] = _dot(p.astype(BF16), v_ref[0, :, cols]).astype(BF16)
    cross = _dot(ctx_scr[...], wo_ref[...])
    h2 = _layer_norm(ALPHA * h + cross, g_ref[...], b_ref[...])

    rows = h2.shape[0]
    comb, g_idx = _route(h2, wr_ref, br_ref)
    lane = lax.broadcasted_iota(jnp.int32, (rows, LANES), 1)
    in_my_group = lane == g_idx
    onehot = jnp.where(in_my_group, 1.0, 0.0).astype(BF16)
    tri = jnp.where(lax.broadcasted_iota(jnp.int32, (rows, rows), 0)
                    >= lax.broadcasted_iota(jnp.int32, (rows, rows), 1), 1.0, 0.0).astype(BF16)
    seen = _dot(tri, onehot) + run_scr[...]
    rank = jnp.sum(jnp.where(in_my_group, seen - 1.0, 0.0), axis=-1, keepdims=True)
    run_scr[...] = seen[rows - 1:rows, :]
    route = jnp.where(lane == ROUTE_GROUP_LANE, g_idx.astype(F32),
                      jnp.where(lane == ROUTE_RANK_LANE, rank, comb))
    haug_ref[:, 0:d] = h2
    haug_ref[:, d:d + LANES] = route
    route_ref[...] = route
    count_ref[...] = jnp.broadcast_to(seen[rows - 1:rows, :], count_ref.shape)


def _cross_call(h, kt, v, wq, wo, g, b, wr, br, seq):
    n, d = h.shape
    m = v.shape[1]
    tiles_per_seq = seq // ROW_TILE
    row = lambda w: pl.BlockSpec((ROW_TILE, w), lambda i: (i, 0))
    vec = _const_spec((1, d))
    return pl.pallas_call(
        _cross_kernel, grid=(n // ROW_TILE,),
        in_specs=[row(d), pl.BlockSpec((1, d, m), lambda i: (i // tiles_per_seq, 0, 0)),
                  pl.BlockSpec((1, m, d), lambda i: (i // tiles_per_seq, 0, 0)),
                  _const_spec((d, d)), _const_spec((d, d)), vec, vec,
                  _const_spec(wr.shape), _const_spec(br.shape)],
        out_specs=[row(d + LANES), row(LANES), _const_spec((SUBLANES, LANES))],
        out_shape=[jax.ShapeDtypeStruct((n, d + LANES), F32), jax.ShapeDtypeStruct((n, LANES), F32),
                   jax.ShapeDtypeStruct((SUBLANES, LANES), F32)],
        scratch_shapes=[pltpu.VMEM((ROW_TILE, d), BF16), pltpu.VMEM((1, LANES), F32)],
        compiler_params=_params("arbitrary"), name="cross_attn")(
            h, kt, v, wq, wo, g.reshape(1, d), b.reshape(1, d), wr, br)


def _first_max(x, valid, lane):
    xm = jnp.where(valid, x, -jnp.inf)
    mx = jnp.max(xm, axis=-1, keepdims=True)
    idx = jnp.min(jnp.where(valid & (xm == mx), lane, LANES), axis=-1, keepdims=True)
    return xm, mx, idx


def _route(h, wr_ref, br_ref):
    rows = h.shape[0]
    logits = _dot3(h, wr_ref[...], _dot) + br_ref[...]
    lane = lax.broadcasted_iota(jnp.int32, (rows, LANES), 1)
    g_logits, g_max, g_idx = _first_max(logits, lane < N_GROUPS, lane)
    g_w = 1.0 / jnp.sum(jnp.exp(g_logits - g_max), axis=-1, keepdims=True)
    expert = lane - N_GROUPS
    group_of = lax.shift_right_arithmetic(expert, int(math.log2(EXPERTS_PER_GROUP)))
    in_group = (expert >= 0) & (expert < N_EXPERTS) & (group_of == g_idx)
    e_logits, e_max, _ = _first_max(logits, in_group, lane)
    e_exp = jnp.exp(e_logits - e_max)
    e_prob = e_exp / jnp.sum(e_exp, axis=-1, keepdims=True)
    _, p1, i1 = _first_max(e_prob, in_group, lane)
    _, p2, i2 = _first_max(e_prob, in_group & (lane != i1), lane)
    denom = p1 + p2
    local = jnp.where(lane == i1, p1 / denom, jnp.where(lane == i2, p2 / denom, 0.0))
    return local * g_w, g_idx


def _router_tables(wr_g, br_g, wr_e, br_e):
    d = wr_g.shape[0]
    pad = LANES - N_GROUPS - N_EXPERTS
    wr = jnp.concatenate([wr_g, wr_e, jnp.zeros((d, pad), F32)], axis=1)
    br = jnp.concatenate([br_g, br_e, jnp.zeros((pad,), F32)]).reshape(1, LANES)
    return wr, br


def _permute_kernel(scatter, pos_ref, src_ref, dst_ref, sem):
    base = pl.program_id(0) * PERM_ROWS

    def row_copy(r):
        there = pl.ds(pos_ref[base + r], 1)
        here = pl.ds(r, 1)
        if scatter:
            return pltpu.make_async_copy(src_ref.at[here, :], dst_ref.at[there, :], sem)
        return pltpu.make_async_copy(src_ref.at[there, :], dst_ref.at[here, :], sem)

    for r in range(PERM_ROWS):
        row_copy(r).start(priority=r % 2)

    def wait(i, carry):
        for k in range(PERM_UNROLL):
            row_copy(i * PERM_UNROLL + k).wait()
        return carry

    lax.fori_loop(0, PERM_ROWS // PERM_UNROLL, wait, 0)


def _permute_call(src, pos, scatter, name):
    n, w = src.shape
    block = pl.BlockSpec((PERM_ROWS, w), lambda i, pos: (i, 0))
    hbm = pl.BlockSpec(memory_space=pl.ANY)
    grid_spec = pltpu.PrefetchScalarGridSpec(
        num_scalar_prefetch=1, grid=(n // PERM_ROWS,),
        in_specs=[block if scatter else hbm], out_specs=hbm if scatter else block,
        scratch_shapes=[pltpu.SemaphoreType.DMA(())])
    return pl.pallas_call(
        functools.partial(_permute_kernel, scatter), grid_spec=grid_spec,
        out_shape=jax.ShapeDtypeStruct((n, w), src.dtype),
        compiler_params=_params("arbitrary"), name=name)(pos, src)


def _moe_kernel(tile_ref, group_ref, flag_ref, x_ref, wg_ref, wu_ref, wd_ref, g_ref, b_ref, o_ref,
                hb_scr, acc_scr):
    p = pl.program_id(0)
    e = pl.program_id(1)
    d = o_ref.shape[1]
    flags = flag_ref[p]

    @pl.when(((flags & PAIR_FIRST) != 0) & (e == 0))
    def _():
        hb_scr[...] = x_ref[:, 0:d].astype(BF16)
        acc_scr[...] = jnp.zeros_like(acc_scr)

    @pl.when((flags & PAIR_VALID) != 0)
    def _():
        hb = hb_scr[...]
        he = jax.nn.silu(_dot(hb, wg_ref[0])) * _dot(hb, wu_ref[0])
        y = _dot(he.astype(BF16), wd_ref[0])
        route = x_ref[:, d:d + LANES]
        lane = lax.broadcasted_iota(jnp.int32, route.shape, 1)
        mine = lane == N_GROUPS + group_ref[p] * EXPERTS_PER_GROUP + e
        weight = jnp.sum(jnp.where(mine, route, 0.0), axis=-1, keepdims=True)
        acc_scr[...] += weight * y

    @pl.when(((flags & PAIR_LAST) != 0) & (e == pl.num_programs(1) - 1))
    def _():
        o_ref[...] = _layer_norm(ALPHA * x_ref[:, 0:d] + acc_scr[...], g_ref[...], b_ref[...])


def _moe_call(xs, pair_tile, pair_group, pair_flags, w_gate, w_up, w_down, g, b):
    n, daug = xs.shape
    d = daug - LANES
    _, _, de = w_gate.shape
    last_e = EXPERTS_PER_GROUP - 1

    def expert(p, e, tile, group, flags):
        return group[p] * EXPERTS_PER_GROUP + jnp.where((flags[p] & PAIR_VALID) != 0, e, last_e)

    vec = pl.BlockSpec((1, d), lambda p, e, tile, group, flags: (0, 0))
    grid_spec = pltpu.PrefetchScalarGridSpec(
        num_scalar_prefetch=3, grid=(pair_tile.shape[0], EXPERTS_PER_GROUP),
        in_specs=[pl.BlockSpec((MOE_TILE, daug), lambda p, e, tile, group, flags: (tile[p], 0)),
                  pl.BlockSpec((1, d, de), lambda p, e, *s: (expert(p, e, *s), 0, 0)),
                  pl.BlockSpec((1, d, de), lambda p, e, *s: (expert(p, e, *s), 0, 0)),
                  pl.BlockSpec((1, de, d), lambda p, e, *s: (expert(p, e, *s), 0, 0)), vec, vec],
        out_specs=pl.BlockSpec((MOE_TILE, d), lambda p, e, tile, group, flags: (tile[p], 0)),
        scratch_shapes=[pltpu.VMEM((MOE_TILE, d), BF16), pltpu.VMEM((MOE_TILE, d), F32)])
    return pl.pallas_call(
        _moe_kernel, grid_spec=grid_spec, out_shape=jax.ShapeDtypeStruct((n, d), F32),
        compiler_params=_params("arbitrary", "arbitrary"), name="moe")(
            pair_tile, pair_group, pair_flags, xs, w_gate, w_up, w_down, g.reshape(1, d), b.reshape(1, d))


def _moe_tables(route, counts, n):
    gid = route[:, ROUTE_GROUP_LANE].astype(jnp.int32)
    rank = route[:, ROUTE_RANK_LANE].astype(jnp.int32)
    cnt = counts[0, :N_GROUPS].astype(jnp.int32)
    ends = jnp.cumsum(cnt)
    starts = ends - cnt
    pos = starts[gid] + rank
    tiles = n // MOE_TILE
    pairs = tiles + N_GROUPS - 1
    lo = jnp.arange(tiles, dtype=jnp.int32)[:, None] * MOE_TILE
    active = ((starts[None, :] < lo + MOE_TILE) & (ends[None, :] > lo)).reshape(-1)
    n_active = jnp.sum(active.astype(jnp.int32))
    idx = jnp.nonzero(active, size=pairs, fill_value=0)[0].astype(jnp.int32)
    valid = jnp.arange(pairs, dtype=jnp.int32) < n_active
    idx = jnp.where(valid, idx, idx[n_active - 1])
    tile = idx // N_GROUPS
    group = idx % N_GROUPS
    prev_tile = jnp.concatenate([jnp.full((1,), -1, jnp.int32), tile[:-1]])
    next_tile = jnp.concatenate([tile[1:], jnp.full((1,), -1, jnp.int32)])
    next_valid = jnp.concatenate([valid[1:], jnp.zeros((1,), bool)])
    first = valid & (tile != prev_tile)
    last = valid & ((tile != next_tile) | ~next_valid)
    flags = (valid.astype(jnp.int32) * PAIR_VALID + first.astype(jnp.int32) * PAIR_FIRST
             + last.astype(jnp.int32) * PAIR_LAST)
    return pos, tile, group, flags


def _moe_full(haug, route, counts, w_gate, w_up, w_down, g, b):
    n = haug.shape[0]
    pos, tile, group, flags = _moe_tables(route, counts, n)
    xs = _permute_call(haug, pos, True, "moe_dispatch")
    ys = _moe_call(xs, tile, group, flags, w_gate.astype(BF16), w_up.astype(BF16),
                   w_down.astype(BF16), g, b)
    return _permute_call(ys, pos, False, "moe_return")


def kernel(x, mem, ln0_g, ln0_b, w_in, mix_g, w_out, ssm_a_re, ssm_a_im, ssm_b_re, ssm_b_im, ssm_c_re, ssm_c_im, ssm_d, ssm_log_dt, ssm_w_glu, ssm_b_glu, lru_conv_w, lru_conv_b, lru_w_a, lru_b_a, lru_w_x, lru_b_x, lru_lam, ln1_g, ln1_b, mem_wq, mem_wk, mem_wv, mem_wo, ln2_g, ln2_b, moe_wr_g, moe_br_g, moe_wr_e, moe_br_e, moe_w_gate, moe_w_up, moe_w_down, ln3_g, ln3_b):
    bsz, seq, d = x.shape
    depth = w_in.shape[0]
    assert seq % MOBA_BLOCK == 0 and seq % SCAN_CHUNK == 0 and seq % ROW_TILE == 0
    assert seq // MOBA_BLOCK <= SUBLANES
    h = x.reshape(bsz * seq, d)
    for l in range(depth):
        if l == 0:
            h, q, k, v, u, xl, gl = _in_proj_call(h, w_in[l].astype(BF16), ln=(ln0_g, ln0_b))
        else:
            q, k, v, u, xl, gl = _in_proj_call(h, w_in[l].astype(BF16))
        y_attn = _moba_call(q, k, v, bsz, seq)
        tables = _s5_tables(ssm_a_re[l], ssm_a_im[l], ssm_b_re[l], ssm_b_im[l], ssm_c_re[l],
                            ssm_c_im[l], ssm_log_dt[l])
        y_ssm = _s5_call(u, tables, ssm_d[l], ssm_w_glu[l], ssm_b_glu[l], bsz, seq)
        y_lru = _lru_call(xl, gl, lru_conv_w[l], lru_conv_b[l], lru_w_a[l], lru_b_a[l],
                          lru_w_x[l], lru_b_x[l], lru_lam[l], bsz, seq)
        h = _out_proj_call(y_attn, y_ssm, y_lru, h, mix_g[l], w_out[l].astype(BF16),
                           ln1_g[l], ln1_b[l])
        kt, vm = _mem_kv_call(mem, mem_wk[l].astype(BF16), mem_wv[l].astype(BF16))
        wr, br = _router_tables(moe_wr_g[l], moe_br_g[l], moe_wr_e[l], moe_br_e[l])
        haug, route, counts = _cross_call(h, kt, vm, mem_wq[l].astype(BF16), mem_wo[l].astype(BF16),
                                          ln2_g[l], ln2_b[l], wr, br, seq)
        h = _moe_full(haug, route, counts, moe_w_gate[l], moe_w_up[l], moe_w_down[l],
                      ln3_g[l], ln3_b[l])
    return h.reshape(bsz, seq, d)
```

```python
import functools
import math

import jax
import jax.numpy as jnp
from jax import lax
from jax.experimental import pallas as pl
from jax.experimental.pallas import tpu as pltpu

F32 = jnp.float32
BF16 = jnp.bfloat16

D_ATTN = 512
D_SSM = 256
D_LRU = 256
SSM_GROUPS = 16
SSM_GROUP = 16
SSM_STATE = 64
D_STATE = SSM_GROUPS * SSM_STATE
LRU_HEADS = 4
CONV_WIDTH = 4
LRU_C = 8.0
ATTN_HEADS = 8
ATTN_HEAD_DIM = 64
HEADS_PER_STEP = 4
MOBA_BLOCK = 256
MOBA_TOPK = 3
MEM_HEADS = 4
N_GROUPS = 4
EXPERTS_PER_GROUP = 4
N_EXPERTS = 16
DEPTH = 2
ALPHA = (2.0 * DEPTH) ** 0.25
LN_EPS = 1e-5
RMS_EPS = 1e-6
NEG_INF = -1e30

SUBLANES = 8
LANES = 128
VMEM_LIMIT = 48 * 1024 * 1024
ROW_TILE = 512
MOE_TILE = 1024
PERM_ROWS = 1024
PERM_UNROLL = 16
ROUTE_GROUP_LANE = 0
ROUTE_RANK_LANE = 1
PAIR_VALID, PAIR_FIRST, PAIR_LAST = 1, 2, 4
SCAN_CHUNK = 512


def _params(*sem):
    return pltpu.CompilerParams(dimension_semantics=sem, vmem_limit_bytes=VMEM_LIMIT)


def _const_spec(shape):
    zeros = (0,) * len(shape)
    return pl.BlockSpec(shape, lambda *_: zeros)


def _layer_norm(x, g, b):
    mu = jnp.mean(x, axis=-1, keepdims=True)
    xc = x - mu
    var = jnp.mean(xc * xc, axis=-1, keepdims=True)
    return xc * lax.rsqrt(var + LN_EPS) * g + b


def _rms(x):
    return x * lax.rsqrt(jnp.mean(x * x, axis=-1, keepdims=True) + RMS_EPS)


def _dot(a, b):
    return jnp.dot(a, b, preferred_element_type=F32)


def _dot_nt(a, b):
    return lax.dot_general(a, b, (((1,), (1,)), ((), ())), preferred_element_type=F32)


def _split_bf16(x):
    hi = x.astype(BF16)
    lo = (x - hi.astype(F32)).astype(BF16)
    return hi, lo


def _dot3(a, b, dot):
    ah, al = _split_bf16(a)
    bh, bl = _split_bf16(b)
    return dot(ah, bh) + dot(ah, bl) + dot(al, bh)


_IN_SPLITS = (D_ATTN, D_ATTN, D_ATTN, D_SSM, D_LRU, D_LRU)


def _project(hb, w_ref, out_refs):
    start = 0
    for o_ref, width in zip(out_refs, _IN_SPLITS):
        o_ref[...] = _dot(hb, w_ref[:, start:start + width])
        start += width


def _in_proj_kernel(h_ref, w_ref, *out_refs):
    _project(h_ref[...].astype(BF16), w_ref, out_refs)


def _ln_in_proj_kernel(x_ref, g_ref, b_ref, w_ref, h_ref, *out_refs):
    h = _layer_norm(x_ref[...], g_ref[...], b_ref[...])
    h_ref[...] = h
    _project(h.astype(BF16), w_ref, out_refs)


def _in_proj_call(h, w_in, ln=None):
    n, d = h.shape
    row = lambda w: pl.BlockSpec((ROW_TILE, w), lambda i: (i, 0))
    widths = _IN_SPLITS if ln is None else (d,) + _IN_SPLITS
    vecs = [] if ln is None else [v.reshape(1, d) for v in ln]
    return pl.pallas_call(
        _in_proj_kernel if ln is None else _ln_in_proj_kernel, grid=(n // ROW_TILE,),
        in_specs=[row(d)] + [_const_spec((1, d))] * len(vecs) + [_const_spec(w_in.shape)],
        out_specs=[row(w) for w in widths],
        out_shape=[jax.ShapeDtypeStruct((n, w), F32) for w in widths],
        compiler_params=_params("parallel"), name="in_proj")(h, *vecs, w_in)


AUG_ROWS = 16
ALIBI_LANE0 = SUBLANES
GATE_STRIDE = 32


def _moba_prepare(k_ref, v_ref, kt_scr, vb_scr, kmh_scr, kml_scr, nb):
    blk = MOBA_BLOCK
    width = k_ref.shape[1]
    arow = lax.broadcasted_iota(jnp.int32, (AUG_ROWS, blk), 0)
    akey = lax.broadcasted_iota(jnp.int32, (AUG_ROWS, blk), 1).astype(F32)
    lane_head = lax.shift_right_logical(lax.broadcasted_iota(jnp.int32, (1, width), 1),
                                        int(math.log2(ATTN_HEAD_DIM)))
    gate_rows = []
    for j in range(nb):
        kj = k_ref[j * blk:(j + 1) * blk, :]
        ktj = kj.T.astype(BF16)
        aug = jnp.where(arow == ALIBI_LANE0, float(blk * j),
                        jnp.where(arow == ALIBI_LANE0 + 1, akey, jnp.where(arow == j, 1.0, 0.0)))
        aug = aug.astype(BF16)
        kt_scr[0, j] = ktj
        kt_scr[0, j, LANES:LANES + AUG_ROWS, :] = aug
        kt_scr[1, j] = ktj
        kt_scr[1, j, 0:AUG_ROWS, :] = aug
        vj = v_ref[j * blk:(j + 1) * blk, :]
        vlane = lax.broadcasted_iota(jnp.int32, vj.shape, 1)
        vb_scr[0, j * blk:(j + 1) * blk, :] = jnp.where(vlane == LANES, 1.0, vj).astype(BF16)
        vb_scr[1, j * blk:(j + 1) * blk, :] = jnp.where(vlane == 0, 1.0, vj).astype(BF16)
        gate_rows.append(jnp.mean(kj, axis=0, keepdims=True))
    gate_rows.append(jnp.zeros((GATE_STRIDE - nb, width), F32))
    kmean = jnp.concatenate(gate_rows, axis=0)
    table = jnp.concatenate([jnp.where(lane_head == h, kmean, 0.0) for h in range(HEADS_PER_STEP)], axis=0)
    hi, lo = _split_bf16(table)
    kmh_scr[...] = hi
    kml_scr[...] = lo


def _moba_block(c, hg, slopes_ref, q_ref, o_ref, kt_scr, vb_scr, kmh_scr, kml_scr, nb):
    blk = MOBA_BLOCK
    width = q_ref.shape[1]
    q = q_ref[...]
    lane_head = lax.shift_right_logical(lax.broadcasted_iota(jnp.int32, (blk, width), 1),
                                        int(math.log2(ATTN_HEAD_DIM)))
    col = lax.broadcasted_iota(jnp.int32, (blk, LANES), 1)
    is_alibi = (col >= ALIBI_LANE0) & (col < ALIBI_LANE0 + 2)
    causal = (lax.broadcasted_iota(jnp.int32, (blk, blk), 0)
              >= lax.broadcasted_iota(jnp.int32, (blk, blk), 1))
    zeros = jnp.zeros((blk, LANES), F32)
    if c > MOBA_TOPK:
        qhi, qlo = _split_bf16(q)
        gate_t = (_dot_nt(kmh_scr[...], qhi) + _dot_nt(kml_scr[...], qhi)
                  + _dot_nt(kmh_scr[...], qlo))
        jrow = lax.broadcasted_iota(jnp.int32, (SUBLANES, blk), 0)
        pieces = []
        for hh in range(HEADS_PER_STEP):
            gate = jnp.where(jrow < c, gate_t[GATE_STRIDE * hh:GATE_STRIDE * hh + SUBLANES, :], NEG_INF)
            rank = jnp.zeros((SUBLANES, blk), jnp.int32)
            for jp in range(c):
                gj = jnp.broadcast_to(gate[jp:jp + 1, :], (SUBLANES, blk))
                beats = (gj > gate) | ((gj == gate) & (jrow > jp))
                rank = rank + beats.astype(jnp.int32)
            pieces.append(jnp.where((jrow < c) & (rank >= MOBA_TOPK), NEG_INF, 0.0))
            pieces.append(jnp.zeros((GATE_STRIDE - SUBLANES, blk), F32))
        penalty_all = jnp.concatenate(pieces, axis=0).T
    out = jnp.zeros((blk, width), F32)

    def score_tiles(hh):
        mine = lane_head == hh
        slope = slopes_ref[hg * HEADS_PER_STEP + hh]
        extra = jnp.where(is_alibi, slope, 0.0)
        if c > MOBA_TOPK:
            penalty = penalty_all if hh == 0 else pltpu.roll(penalty_all, LANES - GATE_STRIDE * hh, 1)
            extra = jnp.where(col < SUBLANES, penalty, extra)
        extra = jnp.concatenate([zeros, extra] if hh < HEADS_PER_STEP // 2 else [extra, zeros], axis=1)
        qa = jnp.where(mine, q * ATTN_HEAD_DIM ** -0.5, extra).astype(BF16)
        var = 0 if hh < HEADS_PER_STEP // 2 else 1
        tiles = [_dot(qa, kt_scr[var, j]) for j in range(c)]
        tiles.append(jnp.where(causal, _dot(qa, kt_scr[var, c]), NEG_INF))
        return tiles

    head_tiles = [score_tiles(hh) for hh in range(HEADS_PER_STEP)]
    head_max = []
    for tiles in head_tiles:
        mx = tiles[0]
        for t in tiles[1:]:
            mx = jnp.maximum(mx, t)
        head_max.append(jnp.max(mx, axis=-1, keepdims=True))
    for hh in range(HEADS_PER_STEP):
        mine = lane_head == hh
        var = 0 if hh < HEADS_PER_STEP // 2 else 1
        tiles = head_tiles[hh]
        m = head_max[hh]
        ctx = None
        for j, t in enumerate(tiles):
            part = _dot(jnp.exp(t - m).astype(BF16), vb_scr[var, j * blk:(j + 1) * blk, :])
            ctx = part if ctx is None else ctx + part
        ones_lane = LANES if var == 0 else 0
        out = jnp.where(mine, ctx / ctx[:, ones_lane:ones_lane + 1], out)
    o_ref[...] = out


def _moba_kernel(slopes_ref, q_ref, k_ref, v_ref, o_ref, kt_scr, vb_scr, kmh_scr, kml_scr):
    hg = pl.program_id(1)
    i = pl.program_id(2)
    nb = vb_scr.shape[1] // MOBA_BLOCK

    @pl.when(i == 0)
    def _():
        _moba_prepare(k_ref, v_ref, kt_scr, vb_scr, kmh_scr, kml_scr, nb)

    for c in range(nb):
        @pl.when(i == c)
        def _():
            _moba_block(c, hg, slopes_ref, q_ref, o_ref, kt_scr, vb_scr, kmh_scr, kml_scr, nb)


def _moba_call(q, k, v, bsz, seq):
    nb = seq // MOBA_BLOCK
    width = HEADS_PER_STEP * ATTN_HEAD_DIM
    slopes = jnp.asarray([2.0 ** (-8.0 * (h + 1) / ATTN_HEADS) for h in range(ATTN_HEADS)], F32)
    qspec = pl.BlockSpec((MOBA_BLOCK, width), lambda b, g, i: (b * nb + i, g))
    kvspec = pl.BlockSpec((seq, width), lambda b, g, i: (b, g))
    return pl.pallas_call(
        _moba_kernel, grid=(bsz, D_ATTN // width, nb),
        in_specs=[pl.BlockSpec(memory_space=pltpu.SMEM), qspec, kvspec, kvspec],
        out_specs=qspec, out_shape=jax.ShapeDtypeStruct(q.shape, F32),
        scratch_shapes=[pltpu.VMEM((2, nb, width, MOBA_BLOCK), BF16),
                        pltpu.VMEM((2, seq, width), BF16),
                        pltpu.VMEM((LANES, width), BF16),
                        pltpu.VMEM((LANES, width), BF16)],
        compiler_params=_params("parallel", "parallel", "arbitrary"), name="moba")(slopes, q, k, v)


def _s5_kernel(u_ref, bmat_ref, apow_ref, cmat_ref, d_ref, wglu_ref, bglu_ref, o_ref,
               half_scr, up_scr, x_scr, xb_scr, carry_scr):
    c = pl.program_id(1)
    chunk = u_ref.shape[0]
    steps = chunk // SUBLANES
    ns = D_STATE

    @pl.when(c == 0)
    def _():
        carry_scr[...] = jnp.zeros_like(carry_scr)

    halves = [slice(k * LANES, (k + 1) * LANES) for k in range(u_ref.shape[1] // LANES)]
    for k, cols in enumerate(halves):
        half_scr[k] = u_ref[:, cols]
        for tau in range(steps):
            up_scr[tau * SUBLANES:(tau + 1) * SUBLANES, cols] = (
                half_scr[k, pl.ds(tau, SUBLANES, stride=steps), :])
    up = up_scr[...]
    x_scr[...] = _dot(up.astype(BF16), bmat_ref[...])

    a1r = jnp.broadcast_to(apow_ref[0:1, 0:ns], (SUBLANES, ns))
    a1i = jnp.broadcast_to(apow_ref[0:1, ns:2 * ns], (SUBLANES, ns))

    def local_step(tau, state):
        xr, xi = state
        rows = pl.ds(pl.multiple_of(tau * SUBLANES, SUBLANES), SUBLANES)
        xr, xi = (a1r * xr - a1i * xi + x_scr[rows, 0:ns],
                  a1r * xi + a1i * xr + x_scr[rows, ns:2 * ns])
        x_scr[rows, 0:ns] = xr
        x_scr[rows, ns:2 * ns] = xi
        return xr, xi

    zero = jnp.zeros((SUBLANES, ns), F32)
    end_r, end_i = lax.fori_loop(0, steps, local_step, (zero, zero), unroll=4)

    alr = apow_ref[steps - 1:steps, 0:ns]
    ali = apow_ref[steps - 1:steps, ns:2 * ns]
    er = carry_scr[:, 0:ns]
    ei = carry_scr[:, ns:2 * ns]
    enter_r, enter_i = [], []
    for s in range(SUBLANES):
        enter_r.append(er)
        enter_i.append(ei)
        er, ei = (end_r[s:s + 1, :] + alr * er - ali * ei,
                  end_i[s:s + 1, :] + alr * ei + ali * er)
    carry_scr[:, 0:ns] = er
    carry_scr[:, ns:2 * ns] = ei
    cr = jnp.concatenate(enter_r, axis=0)
    ci = jnp.concatenate(enter_i, axis=0)

    def carry_step(pair, _):
        halves_r, halves_i = [], []
        for k in range(2):
            tau = 2 * pair + k
            rows = pl.ds(pl.multiple_of(tau * SUBLANES, SUBLANES), SUBLANES)
            pr = jnp.broadcast_to(apow_ref[pl.ds(tau, 1), 0:ns], (SUBLANES, ns))
            pi = jnp.broadcast_to(apow_ref[pl.ds(tau, 1), ns:2 * ns], (SUBLANES, ns))
            halves_r.append(x_scr[rows, 0:ns] + pr * cr - pi * ci)
            halves_i.append(x_scr[rows, ns:2 * ns] + pr * ci + pi * cr)
        rows2 = pl.ds(pl.multiple_of(pair * 2 * SUBLANES, 2 * SUBLANES), 2 * SUBLANES)
        xb_scr[rows2, 0:ns] = jnp.concatenate(halves_r, axis=0).astype(BF16)
        xb_scr[rows2, ns:2 * ns] = jnp.concatenate(halves_i, axis=0).astype(BF16)
        return 0

    lax.fori_loop(0, steps // 2, carry_step, 0, unroll=2)

    y = _dot(xb_scr[...], cmat_ref[...])
    y = jax.nn.gelu(y + d_ref[...] * up)
    gate = jax.nn.sigmoid(_dot(y.astype(BF16), wglu_ref[...]) + bglu_ref[...])
    up_scr[...] = y * gate
    for k, cols in enumerate(halves):
        for tau in range(steps):
            half_scr[k, pl.ds(tau, SUBLANES, stride=steps), :] = (
                up_scr[tau * SUBLANES:(tau + 1) * SUBLANES, cols])
        o_ref[:, cols] = half_scr[k]


def _complex_mul(ar, ai, br, bi):
    return ar * br - ai * bi, ar * bi + ai * br


def _s5_tables(a_re, a_im, b_re, b_im, c_re, c_im, log_dt):
    g, p, h = b_re.shape
    dt = jnp.exp(log_dt)[:, None]
    mag = jnp.exp(dt * a_re)
    ab_re, ab_im = mag * jnp.cos(dt * a_im), mag * jnp.sin(dt * a_im)
    den = a_re * a_re + a_im * a_im
    nr, ni = ab_re - 1.0, ab_im
    f_re = (nr * a_re + ni * a_im) / den
    f_im = (ni * a_re - nr * a_im) / den
    bb_re = f_re[..., None] * b_re - f_im[..., None] * b_im
    bb_im = f_re[..., None] * b_im + f_im[..., None] * b_re
    eye = jnp.eye(g, dtype=F32)
    bmat_re = jnp.einsum('gph,gk->ghkp', bb_re, eye).reshape(g * h, g * p)
    bmat_im = jnp.einsum('gph,gk->ghkp', bb_im, eye).reshape(g * h, g * p)
    bmat = jnp.concatenate([bmat_re, bmat_im], axis=1).astype(BF16)
    cmat_re = jnp.einsum('ghp,gk->gpkh', c_re, eye).reshape(g * p, g * h)
    cmat_im = jnp.einsum('ghp,gk->gpkh', c_im, eye).reshape(g * p, g * h)
    cmat = jnp.concatenate([cmat_re, -cmat_im], axis=0).astype(BF16)
    pr, pi = ab_re.reshape(1, g * p), ab_im.reshape(1, g * p)
    steps = SCAN_CHUNK // SUBLANES
    while pr.shape[0] < steps:
        nr2, ni2 = _complex_mul(pr, pi, pr[-1:], pi[-1:])
        pr, pi = jnp.concatenate([pr, nr2], axis=0), jnp.concatenate([pi, ni2], axis=0)
    apow = jnp.concatenate([pr[:steps], pi[:steps]], axis=1)
    return bmat, apow, cmat


def _s5_call(u, tables, d_skip, w_glu, b_glu, bsz, seq):
    bmat, apow, cmat = tables
    n, d = u.shape
    nc = seq // SCAN_CHUNK
    row = pl.BlockSpec((SCAN_CHUNK, d), lambda b, c: (b * nc + c, 0))
    return pl.pallas_call(
        _s5_kernel, grid=(bsz, nc),
        in_specs=[row, _const_spec(bmat.shape), _const_spec(apow.shape),
                  _const_spec(cmat.shape), _const_spec((1, d)), _const_spec((d, d)),
                  _const_spec((1, d))],
        out_specs=row, out_shape=jax.ShapeDtypeStruct((n, d), F32),
        scratch_shapes=[pltpu.VMEM((d // LANES, SCAN_CHUNK, LANES), F32),
                        pltpu.VMEM((SCAN_CHUNK, d), F32),
                        pltpu.VMEM((SCAN_CHUNK, 2 * D_STATE), F32),
                        pltpu.VMEM((SCAN_CHUNK, 2 * D_STATE), BF16),
                        pltpu.VMEM((1, 2 * D_STATE), F32)],
        compiler_params=_params("parallel", "arbitrary"), name="s5")(
            u, bmat, apow, cmat, d_skip.reshape(1, d), w_glu.astype(BF16), b_glu.reshape(1, d))


def _lru_kernel(xl_ref, gl_ref, cw_ref, cb_ref, wa_ref, ba_ref, wx_ref, bx_ref, lam_ref, o_ref,
                ext_scr, a_scr, h_scr, carry_scr):
    c = pl.program_id(1)
    chunk = xl_ref.shape[0]
    halo = SUBLANES

    @pl.when(c == 0)
    def _():
        ext_scr[0:halo, :] = jnp.zeros((halo, ext_scr.shape[1]), F32)
        carry_scr[...] = jnp.zeros_like(carry_scr)

    xl = xl_ref[...]
    ext_scr[halo:, :] = xl
    ext = ext_scr[...]
    xc = cb_ref[...] + cw_ref[CONV_WIDTH - 1:CONV_WIDTH, :] * xl
    for back in range(1, CONV_WIDTH):
        shifted = pltpu.roll(ext, back, 0)[halo:, :]
        xc = xc + cw_ref[CONV_WIDTH - 1 - back:CONV_WIDTH - back, :] * shifted
    ext_scr[0:halo, :] = xl[chunk - halo:, :]

    xb = xc.astype(BF16)
    r = jax.nn.sigmoid(_dot(xb, wa_ref[...]) + ba_ref[...])
    gate_in = jax.nn.sigmoid(_dot(xb, wx_ref[...]) + bx_ref[...])
    neg_lam = -lam_ref[...]
    softplus = jnp.maximum(neg_lam, 0.0) + jnp.log1p(jnp.exp(-jnp.abs(neg_lam)))
    log_a = -LRU_C * r * softplus
    a = jnp.exp(log_a)
    one_minus_a2 = jnp.tanh(-log_a) * (1.0 + a * a)
    a_scr[...] = a
    h_scr[...] = jnp.sqrt(one_minus_a2) * (gate_in * xc)

    sub = lax.broadcasted_iota(jnp.int32, (SUBLANES, a.shape[1]), 0)

    def tile_body(t, carry):
        rows = pl.ds(pl.multiple_of(t * SUBLANES, SUBLANES), SUBLANES)
        at = a_scr[rows, :]
        bt = h_scr[rows, :]
        for lvl in range(3):
            keep = sub >= (1 << lvl)
            a_sh = jnp.where(keep, pltpu.roll(at, 1 << lvl, 0), 1.0)
            b_sh = jnp.where(keep, pltpu.roll(bt, 1 << lvl, 0), 0.0)
            bt = bt + at * b_sh
            at = at * a_sh
        ht = bt + at * carry
        h_scr[rows, :] = ht
        return ht[SUBLANES - 1:SUBLANES, :]

    carry_scr[...] = lax.fori_loop(0, chunk // SUBLANES, tile_body, carry_scr[...], unroll=2)
    o_ref[...] = h_scr[...] * jax.nn.gelu(gl_ref[...])


def _block_diag(w):
    heads, di, do = w.shape
    eye = jnp.eye(heads, dtype=w.dtype)
    return jnp.einsum('hij,hk->hikj', w, eye).reshape(heads * di, heads * do)


def _lru_call(xl, gl, conv_w, conv_b, w_a, b_a, w_x, b_x, lam, bsz, seq):
    n, d = xl.shape
    nc = seq // SCAN_CHUNK
    row = pl.BlockSpec((SCAN_CHUNK, d), lambda b, c: (b * nc + c, 0))
    vec = _const_spec((1, d))
    mat = _const_spec((d, d))
    return pl.pallas_call(
        _lru_kernel, grid=(bsz, nc),
        in_specs=[row, row, _const_spec((CONV_WIDTH, d)), vec, mat, vec, mat, vec, vec],
        out_specs=row, out_shape=jax.ShapeDtypeStruct((n, d), F32),
        scratch_shapes=[pltpu.VMEM((SCAN_CHUNK + SUBLANES, d), F32),
                        pltpu.VMEM((SCAN_CHUNK, d), F32),
                        pltpu.VMEM((SCAN_CHUNK, d), F32),
                        pltpu.VMEM((1, d), F32)],
        compiler_params=_params("parallel", "arbitrary"), name="rglru")(
            xl, gl, conv_w, conv_b.reshape(1, d), _block_diag(w_a).astype(BF16), b_a.reshape(1, d),
            _block_diag(w_x).astype(BF16), b_x.reshape(1, d), lam.reshape(1, d))


def _out_proj_kernel(ya_ref, ys_ref, yl_ref, h_ref, mg_ref, w_ref, g_ref, b_ref, o_ref):
    mix = None
    start = 0
    for y_ref in (ya_ref, ys_ref, yl_ref):
        width = y_ref.shape[1]
        y = (_rms(y_ref[...]) * mg_ref[:, start:start + width]).astype(BF16)
        part = _dot(y, w_ref[start:start + width, :])
        mix = part if mix is None else mix + part
        start += width
    o_ref[...] = _layer_norm(ALPHA * h_ref[...] + mix, g_ref[...], b_ref[...])


def _out_proj_call(ya, ys, yl, h, mix_g, w_out, g, b):
    n, d = h.shape
    row = lambda w: pl.BlockSpec((ROW_TILE, w), lambda i: (i, 0))
    vec = _const_spec((1, d))
    return pl.pallas_call(
        _out_proj_kernel, grid=(n // ROW_TILE,),
        in_specs=[row(ya.shape[1]), row(ys.shape[1]), row(yl.shape[1]), row(d), vec,
                  _const_spec(w_out.shape), vec, vec],
        out_specs=row(d), out_shape=jax.ShapeDtypeStruct((n, d), F32),
        compiler_params=_params("parallel"), name="out_proj")(
            ya, ys, yl, h, mix_g.reshape(1, d), w_out, g.reshape(1, d), b.reshape(1, d))


def _mem_kv_kernel(mem_ref, wk_ref, wv_ref, kt_ref, v_ref):
    mb = mem_ref[...].astype(BF16)
    kt_ref[0] = _dot(mb, wk_ref[...]).T.astype(BF16)
    v_ref[0] = _dot(mb, wv_ref[...]).astype(BF16)


def _mem_kv_call(mem, wk, wv):
    bsz, m, d = mem.shape
    return pl.pallas_call(
        _mem_kv_kernel, grid=(bsz,),
        in_specs=[pl.BlockSpec((m, d), lambda b: (b, 0)), _const_spec((d, d)), _const_spec((d, d))],
        out_specs=[pl.BlockSpec((1, d, m), lambda b: (b, 0, 0)),
                   pl.BlockSpec((1, m, d), lambda b: (b, 0, 0))],
        out_shape=[jax.ShapeDtypeStruct((bsz, d, m), BF16), jax.ShapeDtypeStruct((bsz, m, d), BF16)],
        compiler_params=_params("parallel"), name="mem_kv")(mem.reshape(bsz * m, d), wk, wv)


def _cross_kernel(h_ref, kt_ref, v_ref, wq_ref, wo_ref, g_ref, b_ref, wr_ref, br_ref,
                  haug_ref, route_ref, count_ref, ctx_scr, run_scr):
    @pl.when(pl.program_id(0) == 0)
    def _():
        run_scr[...] = jnp.zeros_like(run_scr)

    h = h_ref[...]
    d = h.shape[1]
    hd = d // MEM_HEADS
    q = _dot(h.astype(BF16), wq_ref[...])
    scale = hd ** -0.5
    for hh in range(MEM_HEADS):
        cols = slice(hh * hd, (hh + 1) * hd)
        s = _dot(q[:, cols].astype(BF16), kt_ref[0, cols, :]) * scale
        s = s - jnp.max(s, axis=-1, keepdims=True)
        p = jnp.exp(s)
        p = p / jnp.sum(p, axis=-1, keepdims=True)
        ctx_scr[:, cols] = _dot(p.astype(BF16), v_ref[0, :, cols]).astype(BF16)
    cross = _dot(ctx_scr[...], wo_ref[...])
    h2 = _layer_norm(ALPHA * h + cross, g_ref[...], b_ref[...])

    rows = h2.shape[0]
    comb, g_idx = _route(h2, wr_ref, br_ref)
    lane = lax.broadcasted_iota(jnp.int32, (rows, LANES), 1)
    in_my_group = lane == g_idx
    onehot = jnp.where(in_my_group, 1.0, 0.0).astype(BF16)
    tri = jnp.where(lax.broadcasted_iota(jnp.int32, (rows, rows), 0)
                    >= lax.broadcasted_iota(jnp.int32, (rows, rows), 1), 1.0, 0.0).astype(BF16)
    seen = _dot(tri, onehot) + run_scr[...]
    rank = jnp.sum(jnp.where(in_my_group, seen - 1.0, 0.0), axis=-1, keepdims=True)
    run_scr[...] = seen[rows - 1:rows, :]
    route = jnp.where(lane == ROUTE_GROUP_LANE, g_idx.astype(F32),
                      jnp.where(lane == ROUTE_RANK_LANE, rank, comb))
    haug_ref[:, 0:d] = h2
    haug_ref[:, d:d + LANES] = route
    route_ref[...] = route.T[0:SUBLANES, :]
    count_ref[...] = jnp.broadcast_to(seen[rows - 1:rows, :], count_ref.shape)


def _cross_call(h, kt, v, wq, wo, g, b, wr, br, seq):
    n, d = h.shape
    m = v.shape[1]
    tiles_per_seq = seq // ROW_TILE
    row = lambda w: pl.BlockSpec((ROW_TILE, w), lambda i: (i, 0))
    vec = _const_spec((1, d))
    return pl.pallas_call(
        _cross_kernel, grid=(n // ROW_TILE,),
        in_specs=[row(d), pl.BlockSpec((1, d, m), lambda i: (i // tiles_per_seq, 0, 0)),
                  pl.BlockSpec((1, m, d), lambda i: (i // tiles_per_seq, 0, 0)),
                  _const_spec((d, d)), _const_spec((d, d)), vec, vec,
                  _const_spec(wr.shape), _const_spec(br.shape)],
        out_specs=[row(d + LANES), pl.BlockSpec((SUBLANES, ROW_TILE), lambda i: (0, i)),
                   _const_spec((SUBLANES, LANES))],
        out_shape=[jax.ShapeDtypeStruct((n, d + LANES), F32), jax.ShapeDtypeStruct((SUBLANES, n), F32),
                   jax.ShapeDtypeStruct((SUBLANES, LANES), F32)],
        scratch_shapes=[pltpu.VMEM((ROW_TILE, d), BF16), pltpu.VMEM((1, LANES), F32)],
        compiler_params=_params("arbitrary"), name="cross_attn")(
            h, kt, v, wq, wo, g.reshape(1, d), b.reshape(1, d), wr, br)


def _first_max(x, valid, lane):
    xm = jnp.where(valid, x, -jnp.inf)
    mx = jnp.max(xm, axis=-1, keepdims=True)
    idx = jnp.min(jnp.where(valid & (xm == mx), lane, LANES), axis=-1, keepdims=True)
    return xm, mx, idx


def _route(h, wr_ref, br_ref):
    rows = h.shape[0]
    logits = _dot3(h, wr_ref[...], _dot) + br_ref[...]
    lane = lax.broadcasted_iota(jnp.int32, (rows, LANES), 1)
    g_logits, g_max, g_idx = _first_max(logits, lane < N_GROUPS, lane)
    g_w = 1.0 / jnp.sum(jnp.exp(g_logits - g_max), axis=-1, keepdims=True)
    expert = lane - N_GROUPS
    group_of = lax.shift_right_arithmetic(expert, int(math.log2(EXPERTS_PER_GROUP)))
    in_group = (expert >= 0) & (expert < N_EXPERTS) & (group_of == g_idx)
    e_logits, e_max, _ = _first_max(logits, in_group, lane)
    e_exp = jnp.exp(e_logits - e_max)
    e_prob = e_exp / jnp.sum(e_exp, axis=-1, keepdims=True)
    _, p1, i1 = _first_max(e_prob, in_group, lane)
    _, p2, i2 = _first_max(e_prob, in_group & (lane != i1), lane)
    denom = p1 + p2
    local = jnp.where(lane == i1, p1 / denom, jnp.where(lane == i2, p2 / denom, 0.0))
    return local * g_w, g_idx


def _router_tables(wr_g, br_g, wr_e, br_e):
    d = wr_g.shape[0]
    pad = LANES - N_GROUPS - N_EXPERTS
    wr = jnp.concatenate([wr_g, wr_e, jnp.zeros((d, pad), F32)], axis=1)
    br = jnp.concatenate([br_g, br_e, jnp.zeros((pad,), F32)]).reshape(1, LANES)
    return wr, br


def _permute_kernel(scatter, pos_ref, src_ref, dst_ref, sem):
    base = pl.program_id(0) * PERM_ROWS

    def row_copy(r):
        there = pl.ds(pos_ref[base + r], 1)
        here = pl.ds(r, 1)
        if scatter:
            return pltpu.make_async_copy(src_ref.at[here, :], dst_ref.at[there, :], sem)
        return pltpu.make_async_copy(src_ref.at[there, :], dst_ref.at[here, :], sem)

    for r in range(PERM_ROWS):
        row_copy(r).start(priority=r % 2)

    def wait(i, carry):
        for k in range(PERM_UNROLL):
            row_copy(i * PERM_UNROLL + k).wait()
        return carry

    lax.fori_loop(0, PERM_ROWS // PERM_UNROLL, wait, 0)


def _permute_call(src, pos, scatter, name):
    n, w = src.shape
    block = pl.BlockSpec((PERM_ROWS, w), lambda i, pos: (i, 0))
    hbm = pl.BlockSpec(memory_space=pl.ANY)
    grid_spec = pltpu.PrefetchScalarGridSpec(
        num_scalar_prefetch=1, grid=(n // PERM_ROWS,),
        in_specs=[block if scatter else hbm], out_specs=hbm if scatter else block,
        scratch_shapes=[pltpu.SemaphoreType.DMA(())])
    return pl.pallas_call(
        functools.partial(_permute_kernel, scatter), grid_spec=grid_spec,
        out_shape=jax.ShapeDtypeStruct((n, w), src.dtype),
        compiler_params=_params("arbitrary"), name=name)(pos, src)


def _moe_kernel(tile_ref, group_ref, flag_ref, x_ref, wg_ref, wu_ref, wd_ref, g_ref, b_ref, o_ref,
                hb_scr, acc_scr):
    p = pl.program_id(0)
    e = pl.program_id(1)
    d = o_ref.shape[1]
    flags = flag_ref[p]

    @pl.when(((flags & PAIR_FIRST) != 0) & (e == 0))
    def _():
        hb_scr[...] = x_ref[:, 0:d].astype(BF16)
        acc_scr[...] = jnp.zeros_like(acc_scr)

    @pl.when((flags & PAIR_VALID) != 0)
    def _():
        hb = hb_scr[...]
        he = jax.nn.silu(_dot(hb, wg_ref[0, 0].astype(BF16))) * _dot(hb, wu_ref[0, 0].astype(BF16))
        y = _dot(he.astype(BF16), wd_ref[0, 0].astype(BF16))
        route = x_ref[:, d:d + LANES]
        lane = lax.broadcasted_iota(jnp.int32, route.shape, 1)
        mine = lane == N_GROUPS + group_ref[p] * EXPERTS_PER_GROUP + e
        weight = jnp.sum(jnp.where(mine, route, 0.0), axis=-1, keepdims=True)
        acc_scr[...] += weight * y

    @pl.when(((flags & PAIR_LAST) != 0) & (e == pl.num_programs(1) - 1))
    def _():
        o_ref[...] = _layer_norm(ALPHA * x_ref[:, 0:d] + acc_scr[...], g_ref[...], b_ref[...])


def _moe_call(xs, pair_tile, pair_group, pair_flags, layer, w_gate, w_up, w_down, g, b):
    n, daug = xs.shape
    d = daug - LANES
    de = w_gate.shape[-1]
    last_e = EXPERTS_PER_GROUP - 1

    def expert(p, e, tile, group, flags):
        return group[p] * EXPERTS_PER_GROUP + jnp.where((flags[p] & PAIR_VALID) != 0, e, last_e)

    vec = pl.BlockSpec((1, d), lambda p, e, tile, group, flags: (0, 0))
    grid_spec = pltpu.PrefetchScalarGridSpec(
        num_scalar_prefetch=3, grid=(pair_tile.shape[0], EXPERTS_PER_GROUP),
        in_specs=[pl.BlockSpec((MOE_TILE, daug), lambda p, e, tile, group, flags: (tile[p], 0)),
                  pl.BlockSpec((1, 1, d, de), lambda p, e, *s: (layer, expert(p, e, *s), 0, 0)),
                  pl.BlockSpec((1, 1, d, de), lambda p, e, *s: (layer, expert(p, e, *s), 0, 0)),
                  pl.BlockSpec((1, 1, de, d), lambda p, e, *s: (layer, expert(p, e, *s), 0, 0)), vec, vec],
        out_specs=pl.BlockSpec((MOE_TILE, d), lambda p, e, tile, group, flags: (tile[p], 0)),
        scratch_shapes=[pltpu.VMEM((MOE_TILE, d), BF16), pltpu.VMEM((MOE_TILE, d), F32)])
    return pl.pallas_call(
        _moe_kernel, grid_spec=grid_spec, out_shape=jax.ShapeDtypeStruct((n, d), F32),
        compiler_params=_params("arbitrary", "arbitrary"), name="moe")(
            pair_tile, pair_group, pair_flags, xs, w_gate, w_up, w_down, g.reshape(1, d), b.reshape(1, d))


def _moe_tables(route, counts, n):
    gid = route[ROUTE_GROUP_LANE].astype(jnp.int32)
    rank = route[ROUTE_RANK_LANE].astype(jnp.int32)
    cnt = counts[0, :N_GROUPS].astype(jnp.int32)
    ends = jnp.cumsum(cnt)
    starts = ends - cnt
    pos = starts[gid] + rank
    tiles = n // MOE_TILE
    pairs = tiles + N_GROUPS - 1
    lo = jnp.arange(tiles, dtype=jnp.int32)[:, None] * MOE_TILE
    active = ((starts[None, :] < lo + MOE_TILE) & (ends[None, :] > lo)).reshape(-1)
    n_active = jnp.sum(active.astype(jnp.int32))
    idx = jnp.nonzero(active, size=pairs, fill_value=0)[0].astype(jnp.int32)
    valid = jnp.arange(pairs, dtype=jnp.int32) < n_active
    idx = jnp.where(valid, idx, idx[n_active - 1])
    tile = idx // N_GROUPS
    group = idx % N_GROUPS
    prev_tile = jnp.concatenate([jnp.full((1,), -1, jnp.int32), tile[:-1]])
    next_tile = jnp.concatenate([tile[1:], jnp.full((1,), -1, jnp.int32)])
    next_valid = jnp.concatenate([valid[1:], jnp.zeros((1,), bool)])
    first = valid & (tile != prev_tile)
    last = valid & ((tile != next_tile) | ~next_valid)
    flags = (valid.astype(jnp.int32) * PAIR_VALID + first.astype(jnp.int32) * PAIR_FIRST
             + last.astype(jnp.int32) * PAIR_LAST)
    return pos, tile, group, flags


def _moe_full(haug, route, counts, layer, w_gate, w_up, w_down, g, b):
    n = haug.shape[0]
    pos, tile, group, flags = _moe_tables(route, counts, n)
    xs = _permute_call(haug, pos, True, "moe_dispatch")
    ys = _moe_call(xs, tile, group, flags, layer, w_gate, w_up, w_down, g, b)
    return _permute_call(ys, pos, False, "moe_return")


def kernel(x, mem, ln0_g, ln0_b, w_in, mix_g, w_out, ssm_a_re, ssm_a_im, ssm_b_re, ssm_b_im, ssm_c_re, ssm_c_im, ssm_d, ssm_log_dt, ssm_w_glu, ssm_b_glu, lru_conv_w, lru_conv_b, lru_w_a, lru_b_a, lru_w_x, lru_b_x, lru_lam, ln1_g, ln1_b, mem_wq, mem_wk, mem_wv, mem_wo, ln2_g, ln2_b, moe_wr_g, moe_br_g, moe_wr_e, moe_br_e, moe_w_gate, moe_w_up, moe_w_down, ln3_g, ln3_b):
    bsz, seq, d = x.shape
    depth = w_in.shape[0]
    assert seq % MOBA_BLOCK == 0 and seq % SCAN_CHUNK == 0 and seq % ROW_TILE == 0
    assert seq // MOBA_BLOCK <= SUBLANES
    h = x.reshape(bsz * seq, d)
    for l in range(depth):
        if l == 0:
            h, q, k, v, u, xl, gl = _in_proj_call(h, w_in[l].astype(BF16), ln=(ln0_g, ln0_b))
        else:
            q, k, v, u, xl, gl = _in_proj_call(h, w_in[l].astype(BF16))
        y_attn = _moba_call(q, k, v, bsz, seq)
        tables = _s5_tables(ssm_a_re[l], ssm_a_im[l], ssm_b_re[l], ssm_b_im[l], ssm_c_re[l],
                            ssm_c_im[l], ssm_log_dt[l])
        y_ssm = _s5_call(u, tables, ssm_d[l], ssm_w_glu[l], ssm_b_glu[l], bsz, seq)
        y_lru = _lru_call(xl, gl, lru_conv_w[l], lru_conv_b[l], lru_w_a[l], lru_b_a[l],
                          lru_w_x[l], lru_b_x[l], lru_lam[l], bsz, seq)
        h = _out_proj_call(y_attn, y_ssm, y_lru, h, mix_g[l], w_out[l].astype(BF16),
                           ln1_g[l], ln1_b[l])
        kt, vm = _mem_kv_call(mem, mem_wk[l].astype(BF16), mem_wv[l].astype(BF16))
        wr, br = _router_tables(moe_wr_g[l], moe_br_g[l], moe_wr_e[l], moe_br_e[l])
        haug, route, counts = _cross_call(h, kt, vm, mem_wq[l].astype(BF16), mem_wo[l].astype(BF16),
                                          ln2_g[l], ln2_b[l], wr, br, seq)
        h = _moe_full(haug, route, counts, l, moe_w_gate, moe_w_up, moe_w_down, ln3_g[l], ln3_b[l])
    return h.reshape(bsz, seq, d)
```

```python
import functools
import math

import jax
import jax.numpy as jnp
from jax import lax
from jax.experimental import pallas as pl
from jax.experimental.pallas import tpu as pltpu

F32 = jnp.float32
BF16 = jnp.bfloat16

D_ATTN = 512
D_SSM = 256
D_LRU = 256
SSM_GROUPS = 16
SSM_GROUP = 16
SSM_STATE = 64
D_STATE = SSM_GROUPS * SSM_STATE
LRU_HEADS = 4
CONV_WIDTH = 4
LRU_C = 8.0
ATTN_HEADS = 8
ATTN_HEAD_DIM = 64
HEADS_PER_STEP = 4
MOBA_BLOCK = 256
MOBA_TOPK = 3
MEM_HEADS = 4
N_GROUPS = 4
EXPERTS_PER_GROUP = 4
N_EXPERTS = 16
DEPTH = 2
ALPHA = (2.0 * DEPTH) ** 0.25
LN_EPS = 1e-5
RMS_EPS = 1e-6
NEG_INF = -1e30

SUBLANES = 8
LANES = 128
VMEM_LIMIT = 48 * 1024 * 1024
ROW_TILE = 512
MOE_TILE = 1024
PERM_ROWS = 1024
PERM_UNROLL = 16
ROUTE_GROUP_LANE = 0
ROUTE_RANK_LANE = 1
ROUTE_WEIGHT_LANE0 = 8
PAIR_VALID, PAIR_FIRST, PAIR_LAST = 1, 2, 4
SCAN_CHUNK = 512


def _params(*sem):
    return pltpu.CompilerParams(dimension_semantics=sem, vmem_limit_bytes=VMEM_LIMIT)


def _const_spec(shape):
    zeros = (0,) * len(shape)
    return pl.BlockSpec(shape, lambda *_: zeros)


def _layer_norm(x, g, b):
    mu = jnp.mean(x, axis=-1, keepdims=True)
    xc = x - mu
    var = jnp.mean(xc * xc, axis=-1, keepdims=True)
    return xc * lax.rsqrt(var + LN_EPS) * g + b


def _rms(x):
    return x * lax.rsqrt(jnp.mean(x * x, axis=-1, keepdims=True) + RMS_EPS)


def _dot(a, b):
    return jnp.dot(a, b, preferred_element_type=F32)


def _dot_nt(a, b):
    return lax.dot_general(a, b, (((1,), (1,)), ((), ())), preferred_element_type=F32)


def _split_bf16(x):
    hi = x.astype(BF16)
    lo = (x - hi.astype(F32)).astype(BF16)
    return hi, lo


_IN_SPLITS = (D_ATTN, D_ATTN, D_ATTN, D_SSM, D_LRU, D_LRU)


def _project(hb, w_ref, out_refs):
    start = 0
    for o_ref, width in zip(out_refs, _IN_SPLITS):
        o_ref[...] = _dot(hb, w_ref[:, start:start + width])
        start += width


def _in_proj_kernel(h_ref, w_ref, *out_refs):
    _project(h_ref[...].astype(BF16), w_ref, out_refs)


def _ln_in_proj_kernel(x_ref, g_ref, b_ref, w_ref, h_ref, *out_refs):
    h = _layer_norm(x_ref[...], g_ref[...], b_ref[...])
    h_ref[...] = h
    _project(h.astype(BF16), w_ref, out_refs)


def _in_proj_call(h, w_in, ln=None):
    n, d = h.shape
    row = lambda w: pl.BlockSpec((ROW_TILE, w), lambda i: (i, 0))
    widths = _IN_SPLITS if ln is None else (d,) + _IN_SPLITS
    vecs = [] if ln is None else [v.reshape(1, d) for v in ln]
    return pl.pallas_call(
        _in_proj_kernel if ln is None else _ln_in_proj_kernel, grid=(n // ROW_TILE,),
        in_specs=[row(d)] + [_const_spec((1, d))] * len(vecs) + [_const_spec(w_in.shape)],
        out_specs=[row(w) for w in widths],
        out_shape=[jax.ShapeDtypeStruct((n, w), F32) for w in widths],
        compiler_params=_params("parallel"), name="in_proj")(h, *vecs, w_in)


AUG_ROWS = 16
ALIBI_LANE0 = SUBLANES
GATE_STRIDE = 32


def _moba_prepare(k_ref, v_ref, kt_scr, vb_scr, kmh_scr, kml_scr, nb):
    blk = MOBA_BLOCK
    width = k_ref.shape[1]
    arow = lax.broadcasted_iota(jnp.int32, (AUG_ROWS, blk), 0)
    akey = lax.broadcasted_iota(jnp.int32, (AUG_ROWS, blk), 1).astype(F32)
    lane_head = lax.shift_right_logical(lax.broadcasted_iota(jnp.int32, (1, width), 1),
                                        int(math.log2(ATTN_HEAD_DIM)))
    gate_rows = []
    for j in range(nb):
        kj = k_ref[j * blk:(j + 1) * blk, :]
        ktj = kj.T.astype(BF16)
        aug = jnp.where(arow == ALIBI_LANE0, float(blk * j),
                        jnp.where(arow == ALIBI_LANE0 + 1, akey, jnp.where(arow == j, 1.0, 0.0)))
        aug = aug.astype(BF16)
        kt_scr[0, j] = ktj
        kt_scr[0, j, LANES:LANES + AUG_ROWS, :] = aug
        kt_scr[1, j] = ktj
        kt_scr[1, j, 0:AUG_ROWS, :] = aug
        vj = v_ref[j * blk:(j + 1) * blk, :]
        vlane = lax.broadcasted_iota(jnp.int32, vj.shape, 1)
        vb_scr[0, j * blk:(j + 1) * blk, :] = jnp.where(vlane == LANES, 1.0, vj).astype(BF16)
        vb_scr[1, j * blk:(j + 1) * blk, :] = jnp.where(vlane == 0, 1.0, vj).astype(BF16)
        gate_rows.append(jnp.mean(kj, axis=0, keepdims=True))
    gate_rows.append(jnp.zeros((GATE_STRIDE - nb, width), F32))
    kmean = jnp.concatenate(gate_rows, axis=0)
    table = jnp.concatenate([jnp.where(lane_head == h, kmean, 0.0) for h in range(HEADS_PER_STEP)], axis=0)
    hi, lo = _split_bf16(table)
    kmh_scr[...] = hi
    kml_scr[...] = lo


def _moba_block(c, hg, slopes_ref, q_ref, o_ref, kt_scr, vb_scr, kmh_scr, kml_scr, nb):
    blk = MOBA_BLOCK
    width = q_ref.shape[1]
    q = q_ref[...]
    lane_head = lax.shift_right_logical(lax.broadcasted_iota(jnp.int32, (blk, width), 1),
                                        int(math.log2(ATTN_HEAD_DIM)))
    col = lax.broadcasted_iota(jnp.int32, (blk, LANES), 1)
    is_alibi = (col >= ALIBI_LANE0) & (col < ALIBI_LANE0 + 2)
    causal = (lax.broadcasted_iota(jnp.int32, (blk, blk), 0)
              >= lax.broadcasted_iota(jnp.int32, (blk, blk), 1))
    zeros = jnp.zeros((blk, LANES), F32)
    if c > MOBA_TOPK:
        qhi, qlo = _split_bf16(q)
        gate_t = (_dot_nt(kmh_scr[...], qhi) + _dot_nt(kml_scr[...], qhi)
                  + _dot_nt(kmh_scr[...], qlo))
        jrow = lax.broadcasted_iota(jnp.int32, (SUBLANES, blk), 0)
        pieces = []
        for hh in range(HEADS_PER_STEP):
            gate = jnp.where(jrow < c, gate_t[GATE_STRIDE * hh:GATE_STRIDE * hh + SUBLANES, :], NEG_INF)
            rank = jnp.zeros((SUBLANES, blk), jnp.int32)
            for jp in range(c):
                gj = jnp.broadcast_to(gate[jp:jp + 1, :], (SUBLANES, blk))
                beats = (gj > gate) | ((gj == gate) & (jrow > jp))
                rank = rank + beats.astype(jnp.int32)
            pieces.append(jnp.where((jrow < c) & (rank >= MOBA_TOPK), NEG_INF, 0.0))
            pieces.append(jnp.zeros((GATE_STRIDE - SUBLANES, blk), F32))
        penalty_all = jnp.concatenate(pieces, axis=0).T
    out = jnp.zeros((blk, width), F32)

    def score_tiles(hh):
        mine = lane_head == hh
        slope = slopes_ref[hg * HEADS_PER_STEP + hh]
        extra = jnp.where(is_alibi, slope, 0.0)
        if c > MOBA_TOPK:
            penalty = penalty_all if hh == 0 else pltpu.roll(penalty_all, LANES - GATE_STRIDE * hh, 1)
            extra = jnp.where(col < SUBLANES, penalty, extra)
        extra = jnp.concatenate([zeros, extra] if hh < HEADS_PER_STEP // 2 else [extra, zeros], axis=1)
        qa = jnp.where(mine, q * ATTN_HEAD_DIM ** -0.5, extra).astype(BF16)
        var = 0 if hh < HEADS_PER_STEP // 2 else 1
        tiles = [_dot(qa, kt_scr[var, j]) for j in range(c)]
        tiles.append(jnp.where(causal, _dot(qa, kt_scr[var, c]), NEG_INF))
        return tiles

    head_tiles = [score_tiles(hh) for hh in range(HEADS_PER_STEP)]
    head_max = []
    for tiles in head_tiles:
        mx = tiles[0]
        for t in tiles[1:]:
            mx = jnp.maximum(mx, t)
        head_max.append(jnp.max(mx, axis=-1, keepdims=True))
    for hh in range(HEADS_PER_STEP):
        mine = lane_head == hh
        var = 0 if hh < HEADS_PER_STEP // 2 else 1
        tiles = head_tiles[hh]
        m = head_max[hh]
        ctx = None
        for j, t in enumerate(tiles):
            part = _dot(jnp.exp(t - m).astype(BF16), vb_scr[var, j * blk:(j + 1) * blk, :])
            ctx = part if ctx is None else ctx + part
        ones_lane = LANES if var == 0 else 0
        out = jnp.where(mine, ctx / ctx[:, ones_lane:ones_lane + 1], out)
    o_ref[...] = out


def _moba_kernel(slopes_ref, q_ref, k_ref, v_ref, o_ref, kt_scr, vb_scr, kmh_scr, kml_scr):
    hg = pl.program_id(1)
    i = pl.program_id(2)
    nb = vb_scr.shape[1] // MOBA_BLOCK

    @pl.when(i == 0)
    def _():
        _moba_prepare(k_ref, v_ref, kt_scr, vb_scr, kmh_scr, kml_scr, nb)

    for c in range(nb):
        @pl.when(i == c)
        def _():
            _moba_block(c, hg, slopes_ref, q_ref, o_ref, kt_scr, vb_scr, kmh_scr, kml_scr, nb)


def _moba_call(q, k, v, bsz, seq):
    nb = seq // MOBA_BLOCK
    width = HEADS_PER_STEP * ATTN_HEAD_DIM
    slopes = jnp.asarray([2.0 ** (-8.0 * (h + 1) / ATTN_HEADS) for h in range(ATTN_HEADS)], F32)
    qspec = pl.BlockSpec((MOBA_BLOCK, width), lambda b, g, i: (b * nb + i, g))
    kvspec = pl.BlockSpec((seq, width), lambda b, g, i: (b, g))
    return pl.pallas_call(
        _moba_kernel, grid=(bsz, D_ATTN // width, nb),
        in_specs=[pl.BlockSpec(memory_space=pltpu.SMEM), qspec, kvspec, kvspec],
        out_specs=qspec, out_shape=jax.ShapeDtypeStruct(q.shape, F32),
        scratch_shapes=[pltpu.VMEM((2, nb, width, MOBA_BLOCK), BF16),
                        pltpu.VMEM((2, seq, width), BF16),
                        pltpu.VMEM((LANES, width), BF16),
                        pltpu.VMEM((LANES, width), BF16)],
        compiler_params=_params("parallel", "parallel", "arbitrary"), name="moba")(slopes, q, k, v)


def _s5_kernel(u_ref, bmat_ref, apow_ref, cmat_ref, d_ref, wglu_ref, bglu_ref, o_ref,
               half_scr, up_scr, x_scr, xb_scr, carry_scr):
    c = pl.program_id(1)
    chunk = u_ref.shape[0]
    steps = chunk // SUBLANES
    ns = D_STATE

    @pl.when(c == 0)
    def _():
        carry_scr[...] = jnp.zeros_like(carry_scr)

    halves = [slice(k * LANES, (k + 1) * LANES) for k in range(u_ref.shape[1] // LANES)]
    for k, cols in enumerate(halves):
        half_scr[k] = u_ref[:, cols]
        for tau in range(steps):
            up_scr[tau * SUBLANES:(tau + 1) * SUBLANES, cols] = (
                half_scr[k, pl.ds(tau, SUBLANES, stride=steps), :])
    up = up_scr[...]
    x_scr[...] = _dot(up.astype(BF16), bmat_ref[...])

    a1r = jnp.broadcast_to(apow_ref[0:1, 0:ns], (SUBLANES, ns))
    a1i = jnp.broadcast_to(apow_ref[0:1, ns:2 * ns], (SUBLANES, ns))

    def local_step(tau, state):
        xr, xi = state
        rows = pl.ds(pl.multiple_of(tau * SUBLANES, SUBLANES), SUBLANES)
        xr, xi = (a1r * xr - a1i * xi + x_scr[rows, 0:ns],
                  a1r * xi + a1i * xr + x_scr[rows, ns:2 * ns])
        x_scr[rows, 0:ns] = xr
        x_scr[rows, ns:2 * ns] = xi
        return xr, xi

    zero = jnp.zeros((SUBLANES, ns), F32)
    end_r, end_i = lax.fori_loop(0, steps, local_step, (zero, zero), unroll=4)

    alr = apow_ref[steps - 1:steps, 0:ns]
    ali = apow_ref[steps - 1:steps, ns:2 * ns]
    er = carry_scr[:, 0:ns]
    ei = carry_scr[:, ns:2 * ns]
    enter_r, enter_i = [], []
    for s in range(SUBLANES):
        enter_r.append(er)
        enter_i.append(ei)
        er, ei = (end_r[s:s + 1, :] + alr * er - ali * ei,
                  end_i[s:s + 1, :] + alr * ei + ali * er)
    carry_scr[:, 0:ns] = er
    carry_scr[:, ns:2 * ns] = ei
    cr = jnp.concatenate(enter_r, axis=0)
    ci = jnp.concatenate(enter_i, axis=0)

    def carry_step(pair, _):
        halves_r, halves_i = [], []
        for k in range(2):
            tau = 2 * pair + k
            rows = pl.ds(pl.multiple_of(tau * SUBLANES, SUBLANES), SUBLANES)
            pr = jnp.broadcast_to(apow_ref[pl.ds(tau, 1), 0:ns], (SUBLANES, ns))
            pi = jnp.broadcast_to(apow_ref[pl.ds(tau, 1), ns:2 * ns], (SUBLANES, ns))
            halves_r.append(x_scr[rows, 0:ns] + pr * cr - pi * ci)
            halves_i.append(x_scr[rows, ns:2 * ns] + pr * ci + pi * cr)
        rows2 = pl.ds(pl.multiple_of(pair * 2 * SUBLANES, 2 * SUBLANES), 2 * SUBLANES)
        xb_scr[rows2, 0:ns] = jnp.concatenate(halves_r, axis=0).astype(BF16)
        xb_scr[rows2, ns:2 * ns] = jnp.concatenate(halves_i, axis=0).astype(BF16)
        return 0

    lax.fori_loop(0, steps // 2, carry_step, 0, unroll=2)

    y = _dot(xb_scr[...], cmat_ref[...])
    y = jax.nn.gelu(y + d_ref[...] * up)
    gate = jax.nn.sigmoid(_dot(y.astype(BF16), wglu_ref[...]) + bglu_ref[...])
    up_scr[...] = y * gate
    for k, cols in enumerate(halves):
        for tau in range(steps):
            half_scr[k, pl.ds(tau, SUBLANES, stride=steps), :] = (
                up_scr[tau * SUBLANES:(tau + 1) * SUBLANES, cols])
        o_ref[:, cols] = half_scr[k]


def _complex_mul(ar, ai, br, bi):
    return ar * br - ai * bi, ar * bi + ai * br


def _s5_tables(a_re, a_im, b_re, b_im, c_re, c_im, log_dt):
    g, p, h = b_re.shape
    dt = jnp.exp(log_dt)[:, None]
    mag = jnp.exp(dt * a_re)
    ab_re, ab_im = mag * jnp.cos(dt * a_im), mag * jnp.sin(dt * a_im)
    den = a_re * a_re + a_im * a_im
    nr, ni = ab_re - 1.0, ab_im
    f_re = (nr * a_re + ni * a_im) / den
    f_im = (ni * a_re - nr * a_im) / den
    bb_re = f_re[..., None] * b_re - f_im[..., None] * b_im
    bb_im = f_re[..., None] * b_im + f_im[..., None] * b_re
    eye = jnp.eye(g, dtype=F32)
    bmat_re = jnp.einsum('gph,gk->ghkp', bb_re, eye).reshape(g * h, g * p)
    bmat_im = jnp.einsum('gph,gk->ghkp', bb_im, eye).reshape(g * h, g * p)
    bmat = jnp.concatenate([bmat_re, bmat_im], axis=1).astype(BF16)
    cmat_re = jnp.einsum('ghp,gk->gpkh', c_re, eye).reshape(g * p, g * h)
    cmat_im = jnp.einsum('ghp,gk->gpkh', c_im, eye).reshape(g * p, g * h)
    cmat = jnp.concatenate([cmat_re, -cmat_im], axis=0).astype(BF16)
    pr, pi = ab_re.reshape(1, g * p), ab_im.reshape(1, g * p)
    steps = SCAN_CHUNK // SUBLANES
    while pr.shape[0] < steps:
        nr2, ni2 = _complex_mul(pr, pi, pr[-1:], pi[-1:])
        pr, pi = jnp.concatenate([pr, nr2], axis=0), jnp.concatenate([pi, ni2], axis=0)
    apow = jnp.concatenate([pr[:steps], pi[:steps]], axis=1)
    return bmat, apow, cmat


def _s5_call(u, tables, d_skip, w_glu, b_glu, bsz, seq):
    bmat, apow, cmat = tables
    n, d = u.shape
    nc = seq // SCAN_CHUNK
    row = pl.BlockSpec((SCAN_CHUNK, d), lambda b, c: (b * nc + c, 0))
    return pl.pallas_call(
        _s5_kernel, grid=(bsz, nc),
        in_specs=[row, _const_spec(bmat.shape), _const_spec(apow.shape),
                  _const_spec(cmat.shape), _const_spec((1, d)), _const_spec((d, d)),
                  _const_spec((1, d))],
        out_specs=row, out_shape=jax.ShapeDtypeStruct((n, d), F32),
        scratch_shapes=[pltpu.VMEM((d // LANES, SCAN_CHUNK, LANES), F32),
                        pltpu.VMEM((SCAN_CHUNK, d), F32),
                        pltpu.VMEM((SCAN_CHUNK, 2 * D_STATE), F32),
                        pltpu.VMEM((SCAN_CHUNK, 2 * D_STATE), BF16),
                        pltpu.VMEM((1, 2 * D_STATE), F32)],
        compiler_params=_params("parallel", "arbitrary"), name="s5")(
            u, bmat, apow, cmat, d_skip.reshape(1, d), w_glu.astype(BF16), b_glu.reshape(1, d))


def _lru_kernel(xl_ref, gl_ref, cw_ref, cb_ref, wa_ref, ba_ref, wx_ref, bx_ref, lam_ref, o_ref,
                ext_scr, a_scr, h_scr, carry_scr):
    c = pl.program_id(1)
    chunk = xl_ref.shape[0]
    halo = SUBLANES

    @pl.when(c == 0)
    def _():
        ext_scr[0:halo, :] = jnp.zeros((halo, ext_scr.shape[1]), F32)
        carry_scr[...] = jnp.zeros_like(carry_scr)

    xl = xl_ref[...]
    ext_scr[halo:, :] = xl
    ext = ext_scr[...]
    xc = cb_ref[...] + cw_ref[CONV_WIDTH - 1:CONV_WIDTH, :] * xl
    for back in range(1, CONV_WIDTH):
        shifted = pltpu.roll(ext, back, 0)[halo:, :]
        xc = xc + cw_ref[CONV_WIDTH - 1 - back:CONV_WIDTH - back, :] * shifted
    ext_scr[0:halo, :] = xl[chunk - halo:, :]

    xb = xc.astype(BF16)
    r = jax.nn.sigmoid(_dot(xb, wa_ref[...]) + ba_ref[...])
    gate_in = jax.nn.sigmoid(_dot(xb, wx_ref[...]) + bx_ref[...])
    neg_lam = -lam_ref[...]
    softplus = jnp.maximum(neg_lam, 0.0) + jnp.log1p(jnp.exp(-jnp.abs(neg_lam)))
    log_a = -LRU_C * r * softplus
    a = jnp.exp(log_a)
    one_minus_a2 = jnp.tanh(-log_a) * (1.0 + a * a)
    a_scr[...] = a
    h_scr[...] = jnp.sqrt(one_minus_a2) * (gate_in * xc)

    sub = lax.broadcasted_iota(jnp.int32, (SUBLANES, a.shape[1]), 0)

    def tile_body(t, carry):
        rows = pl.ds(pl.multiple_of(t * SUBLANES, SUBLANES), SUBLANES)
        at = a_scr[rows, :]
        bt = h_scr[rows, :]
        for lvl in range(3):
            keep = sub >= (1 << lvl)
            a_sh = jnp.where(keep, pltpu.roll(at, 1 << lvl, 0), 1.0)
            b_sh = jnp.where(keep, pltpu.roll(bt, 1 << lvl, 0), 0.0)
            bt = bt + at * b_sh
            at = at * a_sh
        ht = bt + at * carry
        h_scr[rows, :] = ht
        return ht[SUBLANES - 1:SUBLANES, :]

    carry_scr[...] = lax.fori_loop(0, chunk // SUBLANES, tile_body, carry_scr[...], unroll=2)
    o_ref[...] = h_scr[...] * jax.nn.gelu(gl_ref[...])


def _block_diag(w):
    heads, di, do = w.shape
    eye = jnp.eye(heads, dtype=w.dtype)
    return jnp.einsum('hij,hk->hikj', w, eye).reshape(heads * di, heads * do)


def _lru_call(xl, gl, conv_w, conv_b, w_a, b_a, w_x, b_x, lam, bsz, seq):
    n, d = xl.shape
    nc = seq // SCAN_CHUNK
    row = pl.BlockSpec((SCAN_CHUNK, d), lambda b, c: (b * nc + c, 0))
    vec = _const_spec((1, d))
    mat = _const_spec((d, d))
    return pl.pallas_call(
        _lru_kernel, grid=(bsz, nc),
        in_specs=[row, row, _const_spec((CONV_WIDTH, d)), vec, mat, vec, mat, vec, vec],
        out_specs=row, out_shape=jax.ShapeDtypeStruct((n, d), F32),
        scratch_shapes=[pltpu.VMEM((SCAN_CHUNK + SUBLANES, d), F32),
                        pltpu.VMEM((SCAN_CHUNK, d), F32),
                        pltpu.VMEM((SCAN_CHUNK, d), F32),
                        pltpu.VMEM((1, d), F32)],
        compiler_params=_params("parallel", "arbitrary"), name="rglru")(
            xl, gl, conv_w, conv_b.reshape(1, d), _block_diag(w_a).astype(BF16), b_a.reshape(1, d),
            _block_diag(w_x).astype(BF16), b_x.reshape(1, d), lam.reshape(1, d))


def _out_proj_kernel(ya_ref, ys_ref, yl_ref, h_ref, mg_ref, w_ref, g_ref, b_ref, o_ref):
    mix = None
    start = 0
    for y_ref in (ya_ref, ys_ref, yl_ref):
        width = y_ref.shape[1]
        y = (_rms(y_ref[...]) * mg_ref[:, start:start + width]).astype(BF16)
        part = _dot(y, w_ref[start:start + width, :])
        mix = part if mix is None else mix + part
        start += width
    o_ref[...] = _layer_norm(ALPHA * h_ref[...] + mix, g_ref[...], b_ref[...])


def _out_proj_call(ya, ys, yl, h, mix_g, w_out, g, b):
    n, d = h.shape
    row = lambda w: pl.BlockSpec((ROW_TILE, w), lambda i: (i, 0))
    vec = _const_spec((1, d))
    return pl.pallas_call(
        _out_proj_kernel, grid=(n // ROW_TILE,),
        in_specs=[row(ya.shape[1]), row(ys.shape[1]), row(yl.shape[1]), row(d), vec,
                  _const_spec(w_out.shape), vec, vec],
        out_specs=row(d), out_shape=jax.ShapeDtypeStruct((n, d), F32),
        compiler_params=_params("parallel"), name="out_proj")(
            ya, ys, yl, h, mix_g.reshape(1, d), w_out, g.reshape(1, d), b.reshape(1, d))


def _mem_kv_kernel(mem_ref, wk_ref, wv_ref, kt_ref, v_ref):
    mb = mem_ref[...].astype(BF16)
    kt_ref[0] = _dot(mb, wk_ref[...]).T.astype(BF16)
    v_ref[0] = _dot(mb, wv_ref[...]).astype(BF16)


def _mem_kv_call(mem, wk, wv):
    bsz, m, d = mem.shape
    return pl.pallas_call(
        _mem_kv_kernel, grid=(bsz,),
        in_specs=[pl.BlockSpec((m, d), lambda b: (b, 0)), _const_spec((d, d)), _const_spec((d, d))],
        out_specs=[pl.BlockSpec((1, d, m), lambda b: (b, 0, 0)),
                   pl.BlockSpec((1, m, d), lambda b: (b, 0, 0))],
        out_shape=[jax.ShapeDtypeStruct((bsz, d, m), BF16), jax.ShapeDtypeStruct((bsz, m, d), BF16)],
        compiler_params=_params("parallel"), name="mem_kv")(mem.reshape(bsz * m, d), wk, wv)


def _cross_kernel(h_ref, kt_ref, v_ref, wq_ref, wo_ref, g_ref, b_ref, wr_ref, br_ref,
                  haug_ref, route_ref, count_ref, ctx_scr, run_scr):
    @pl.when(pl.program_id(0) == 0)
    def _():
        run_scr[...] = jnp.zeros_like(run_scr)

    h = h_ref[...]
    d = h.shape[1]
    hd = d // MEM_HEADS
    q = _dot(h.astype(BF16), wq_ref[...])
    scale = hd ** -0.5
    for hh in range(MEM_HEADS):
        cols = slice(hh * hd, (hh + 1) * hd)
        s = _dot(q[:, cols].astype(BF16), kt_ref[0, cols, :]) * scale
        s = s - jnp.max(s, axis=-1, keepdims=True)
        p = jnp.exp(s)
        p = p / jnp.sum(p, axis=-1, keepdims=True)
        ctx_scr[:, cols] = _dot(p.astype(BF16), v_ref[0, :, cols]).astype(BF16)
    cross = _dot(ctx_scr[...], wo_ref[...])
    h2 = _layer_norm(ALPHA * h + cross, g_ref[...], b_ref[...])

    rows = h2.shape[0]
    head, weights = _route(h2, wr_ref, br_ref, run_scr)
    record = jnp.concatenate([head] + weights + [jnp.zeros((LANES - 3 * SUBLANES, rows), F32)], axis=0)
    haug_ref[:, 0:d] = h2
    haug_ref[:, d:d + LANES] = record.T
    route_ref[...] = head
    count_ref[...] = run_scr[...]


def _cross_call(h, kt, v, wq, wo, g, b, wr, br, seq):
    n, d = h.shape
    m = v.shape[1]
    tiles_per_seq = seq // ROW_TILE
    row = lambda w: pl.BlockSpec((ROW_TILE, w), lambda i: (i, 0))
    vec = _const_spec((1, d))
    return pl.pallas_call(
        _cross_kernel, grid=(n // ROW_TILE,),
        in_specs=[row(d), pl.BlockSpec((1, d, m), lambda i: (i // tiles_per_seq, 0, 0)),
                  pl.BlockSpec((1, m, d), lambda i: (i // tiles_per_seq, 0, 0)),
                  _const_spec((d, d)), _const_spec((d, d)), vec, vec,
                  _const_spec(wr.shape), _const_spec(br.shape)],
        out_specs=[row(d + LANES), pl.BlockSpec((SUBLANES, ROW_TILE), lambda i: (0, i)),
                   _const_spec((SUBLANES, LANES))],
        out_shape=[jax.ShapeDtypeStruct((n, d + LANES), F32), jax.ShapeDtypeStruct((SUBLANES, n), F32),
                   jax.ShapeDtypeStruct((SUBLANES, LANES), F32)],
        scratch_shapes=[pltpu.VMEM((ROW_TILE, d), BF16), pltpu.VMEM((SUBLANES, LANES), F32)],
        compiler_params=_params("arbitrary"), name="cross_attn")(
            h, kt, v, wq, wo, g.reshape(1, d), b.reshape(1, d), wr, br)


def _route(h, wr_ref, br_ref, run_scr):
    rows = h.shape[0]
    hhi, hlo = _split_bf16(h)
    whi, wlo = _split_bf16(wr_ref[...])
    logits = _dot_nt(whi, hhi) + _dot_nt(wlo, hhi) + _dot_nt(whi, hlo) + br_ref[...]
    sub = lax.broadcasted_iota(jnp.int32, (SUBLANES, rows), 0)

    def rows_max(tiles):
        return functools.reduce(jnp.maximum, [jnp.max(t, axis=0, keepdims=True) for t in tiles])

    def rows_min(tiles):
        return functools.reduce(jnp.minimum, [jnp.min(t, axis=0, keepdims=True) for t in tiles])

    lg = jnp.where(sub < N_GROUPS, logits[0:SUBLANES], -jnp.inf)
    g_max = rows_max([lg])
    g_idx = rows_min([jnp.where(lg == g_max, sub, SUBLANES)])
    g_w = 1.0 / jnp.sum(jnp.exp(lg - g_max), axis=0, keepdims=True)

    tiles = [logits[SUBLANES * (1 + t):SUBLANES * (2 + t)] for t in range(2)]
    expert = [sub + SUBLANES * t for t in range(2)]
    in_group = [lax.shift_right_logical(e, int(math.log2(EXPERTS_PER_GROUP))) == g_idx for e in expert]
    masked = [jnp.where(s, t, -jnp.inf) for s, t in zip(in_group, tiles)]
    e_max = rows_max(masked)
    e_exp = [jnp.exp(m - e_max) for m in masked]
    e_sum = functools.reduce(jnp.add, [jnp.sum(e, axis=0, keepdims=True) for e in e_exp])
    prob = [e / e_sum for e in e_exp]

    def first_max(valid):
        vals = [jnp.where(v, p, -jnp.inf) for v, p in zip(valid, prob)]
        mx = rows_max(vals)
        idx = rows_min([jnp.where(v & (x == mx), e, N_EXPERTS) for v, x, e in zip(valid, vals, expert)])
        return mx, idx

    p1, i1 = first_max(in_group)
    p2, i2 = first_max([s & (e != i1) for s, e in zip(in_group, expert)])
    denom = p1 + p2
    weights = [jnp.where(e == i1, p1 / denom, jnp.where(e == i2, p2 / denom, 0.0)) * g_w for e in expert]

    onehot = sub == g_idx
    upper = jnp.where(lax.broadcasted_iota(jnp.int32, (rows, rows), 0)
                      <= lax.broadcasted_iota(jnp.int32, (rows, rows), 1), 1.0, 0.0).astype(BF16)
    seen = (_dot(jnp.where(onehot, 1.0, 0.0).astype(BF16), upper)
            + jnp.broadcast_to(run_scr[:, 0:1], (SUBLANES, rows)))
    rank = jnp.sum(jnp.where(onehot, seen - 1.0, 0.0), axis=0, keepdims=True)
    run_scr[...] = jnp.broadcast_to(seen[:, rows - 1:rows], run_scr.shape)
    head = jnp.where(sub == ROUTE_GROUP_LANE, g_idx.astype(F32), jnp.where(sub == ROUTE_RANK_LANE, rank, 0.0))
    return head, weights


def _router_tables(wr_g, br_g, wr_e, br_e):
    d = wr_g.shape[0]
    gap = ROUTE_WEIGHT_LANE0 - N_GROUPS
    tail = 4 * SUBLANES - ROUTE_WEIGHT_LANE0 - N_EXPERTS
    wr = jnp.concatenate([wr_g.T, jnp.zeros((gap, d), F32), wr_e.T, jnp.zeros((tail, d), F32)], axis=0)
    br = jnp.concatenate([br_g, jnp.zeros((gap,), F32), br_e, jnp.zeros((tail,), F32)])
    return wr, jnp.broadcast_to(br[:, None], (br.shape[0], ROW_TILE))


def _permute_kernel(scatter, pos_ref, src_ref, dst_ref, sem):
    base = pl.program_id(0) * PERM_ROWS

    def row_copy(r):
        there = pl.ds(pos_ref[base + r], 1)
        here = pl.ds(r, 1)
        if scatter:
            return pltpu.make_async_copy(src_ref.at[here, :], dst_ref.at[there, :], sem)
        return pltpu.make_async_copy(src_ref.at[there, :], dst_ref.at[here, :], sem)

    for r in range(PERM_ROWS):
        row_copy(r).start(priority=r % 2)

    def wait(i, carry):
        for k in range(PERM_UNROLL):
            row_copy(i * PERM_UNROLL + k).wait()
        return carry

    lax.fori_loop(0, PERM_ROWS // PERM_UNROLL, wait, 0)


def _permute_call(src, pos, scatter, name):
    n, w = src.shape
    block = pl.BlockSpec((PERM_ROWS, w), lambda i, pos: (i, 0))
    hbm = pl.BlockSpec(memory_space=pl.ANY)
    grid_spec = pltpu.PrefetchScalarGridSpec(
        num_scalar_prefetch=1, grid=(n // PERM_ROWS,),
        in_specs=[block if scatter else hbm], out_specs=hbm if scatter else block,
        scratch_shapes=[pltpu.SemaphoreType.DMA(())])
    return pl.pallas_call(
        functools.partial(_permute_kernel, scatter), grid_spec=grid_spec,
        out_shape=jax.ShapeDtypeStruct((n, w), src.dtype),
        compiler_params=_params("arbitrary"), name=name)(pos, src)


def _moe_kernel(tile_ref, group_ref, flag_ref, x_ref, wg_ref, wu_ref, wd_ref, g_ref, b_ref, o_ref,
                hb_scr, acc_scr):
    p = pl.program_id(0)
    e = pl.program_id(1)
    d = o_ref.shape[1]
    flags = flag_ref[p]

    @pl.when(((flags & PAIR_FIRST) != 0) & (e == 0))
    def _():
        hb_scr[...] = x_ref[:, 0:d].astype(BF16)
        acc_scr[...] = jnp.zeros_like(acc_scr)

    @pl.when((flags & PAIR_VALID) != 0)
    def _():
        hb = hb_scr[...]
        he = jax.nn.silu(_dot(hb, wg_ref[0, 0].astype(BF16))) * _dot(hb, wu_ref[0, 0].astype(BF16))
        y = _dot(he.astype(BF16), wd_ref[0, 0].astype(BF16))
        route = x_ref[:, d:d + LANES]
        lane = lax.broadcasted_iota(jnp.int32, route.shape, 1)
        mine = lane == ROUTE_WEIGHT_LANE0 + group_ref[p] * EXPERTS_PER_GROUP + e
        weight = jnp.sum(jnp.where(mine, route, 0.0), axis=-1, keepdims=True)
        acc_scr[...] += weight * y

    @pl.when(((flags & PAIR_LAST) != 0) & (e == pl.num_programs(1) - 1))
    def _():
        o_ref[...] = _layer_norm(ALPHA * x_ref[:, 0:d] + acc_scr[...], g_ref[...], b_ref[...])


def _moe_call(xs, pair_tile, pair_group, pair_flags, layer, w_gate, w_up, w_down, g, b):
    n, daug = xs.shape
    d = daug - LANES
    de = w_gate.shape[-1]
    last_e = EXPERTS_PER_GROUP - 1

    def expert(p, e, tile, group, flags):
        return group[p] * EXPERTS_PER_GROUP + jnp.where((flags[p] & PAIR_VALID) != 0, e, last_e)

    vec = pl.BlockSpec((1, d), lambda p, e, tile, group, flags: (0, 0))
    grid_spec = pltpu.PrefetchScalarGridSpec(
        num_scalar_prefetch=3, grid=(pair_tile.shape[0], EXPERTS_PER_GROUP),
        in_specs=[pl.BlockSpec((MOE_TILE, daug), lambda p, e, tile, group, flags: (tile[p], 0)),
                  pl.BlockSpec((1, 1, d, de), lambda p, e, *s: (layer, expert(p, e, *s), 0, 0)),
                  pl.BlockSpec((1, 1, d, de), lambda p, e, *s: (layer, expert(p, e, *s), 0, 0)),
                  pl.BlockSpec((1, 1, de, d), lambda p, e, *s: (layer, expert(p, e, *s), 0, 0)), vec, vec],
        out_specs=pl.BlockSpec((MOE_TILE, d), lambda p, e, tile, group, flags: (tile[p], 0)),
        scratch_shapes=[pltpu.VMEM((MOE_TILE, d), BF16), pltpu.VMEM((MOE_TILE, d), F32)])
    return pl.pallas_call(
        _moe_kernel, grid_spec=grid_spec, out_shape=jax.ShapeDtypeStruct((n, d), F32),
        compiler_params=_params("arbitrary", "arbitrary"), name="moe")(
            pair_tile, pair_group, pair_flags, xs, w_gate, w_up, w_down, g.reshape(1, d), b.reshape(1, d))


def _moe_tables(route, counts, n):
    gid = route[ROUTE_GROUP_LANE].astype(jnp.int32)
    rank = route[ROUTE_RANK_LANE].astype(jnp.int32)
    cnt = counts[:N_GROUPS, 0].astype(jnp.int32)
    ends = jnp.cumsum(cnt)
    starts = ends - cnt
    pos = starts[gid] + rank
    tiles = n // MOE_TILE
    pairs = tiles + N_GROUPS - 1
    lo = jnp.arange(tiles, dtype=jnp.int32)[:, None] * MOE_TILE
    active = ((starts[None, :] < lo + MOE_TILE) & (ends[None, :] > lo)).reshape(-1)
    n_active = jnp.sum(active.astype(jnp.int32))
    idx = jnp.nonzero(active, size=pairs, fill_value=0)[0].astype(jnp.int32)
    valid = jnp.arange(pairs, dtype=jnp.int32) < n_active
    idx = jnp.where(valid, idx, idx[n_active - 1])
    tile = idx // N_GROUPS
    group = idx % N_GROUPS
    prev_tile = jnp.concatenate([jnp.full((1,), -1, jnp.int32), tile[:-1]])
    next_tile = jnp.concatenate([tile[1:], jnp.full((1,), -1, jnp.int32)])
    next_valid = jnp.concatenate([valid[1:], jnp.zeros((1,), bool)])
    first = valid & (tile != prev_tile)
    last = valid & ((tile != next_tile) | ~next_valid)
    flags = (valid.astype(jnp.int32) * PAIR_VALID + first.astype(jnp.int32) * PAIR_FIRST
             + last.astype(jnp.int32) * PAIR_LAST)
    return pos, tile, group, flags


def _moe_full(haug, route, counts, layer, w_gate, w_up, w_down, g, b):
    n = haug.shape[0]
    pos, tile, group, flags = _moe_tables(route, counts, n)
    xs = _permute_call(haug, pos, True, "moe_dispatch")
    ys = _moe_call(xs, tile, group, flags, layer, w_gate, w_up, w_down, g, b)
    return _permute_call(ys, pos, False, "moe_return")


def kernel(x, mem, ln0_g, ln0_b, w_in, mix_g, w_out, ssm_a_re, ssm_a_im, ssm_b_re, ssm_b_im, ssm_c_re, ssm_c_im, ssm_d, ssm_log_dt, ssm_w_glu, ssm_b_glu, lru_conv_w, lru_conv_b, lru_w_a, lru_b_a, lru_w_x, lru_b_x, lru_lam, ln1_g, ln1_b, mem_wq, mem_wk, mem_wv, mem_wo, ln2_g, ln2_b, moe_wr_g, moe_br_g, moe_wr_e, moe_br_e, moe_w_gate, moe_w_up, moe_w_down, ln3_g, ln3_b):
    bsz, seq, d = x.shape
    depth = w_in.shape[0]
    assert seq % MOBA_BLOCK == 0 and seq % SCAN_CHUNK == 0 and seq % ROW_TILE == 0
    assert seq // MOBA_BLOCK <= SUBLANES
    h = x.reshape(bsz * seq, d)
    for l in range(depth):
        if l == 0:
            h, q, k, v, u, xl, gl = _in_proj_call(h, w_in[l].astype(BF16), ln=(ln0_g, ln0_b))
        else:
            q, k, v, u, xl, gl = _in_proj_call(h, w_in[l].astype(BF16))
        y_attn = _moba_call(q, k, v, bsz, seq)
        tables = _s5_tables(ssm_a_re[l], ssm_a_im[l], ssm_b_re[l], ssm_b_im[l], ssm_c_re[l],
                            ssm_c_im[l], ssm_log_dt[l])
        y_ssm = _s5_call(u, tables, ssm_d[l], ssm_w_glu[l], ssm_b_glu[l], bsz, seq)
        y_lru = _lru_call(xl, gl, lru_conv_w[l], lru_conv_b[l], lru_w_a[l], lru_b_a[l],
                          lru_w_x[l], lru_b_x[l], lru_lam[l], bsz, seq)
        h = _out_proj_call(y_attn, y_ssm, y_lru, h, mix_g[l], w_out[l].astype(BF16),
                           ln1_g[l], ln1_b[l])
        kt, vm = _mem_kv_call(mem, mem_wk[l].astype(BF16), mem_wv[l].astype(BF16))
        wr, br = _router_tables(moe_wr_g[l], moe_br_g[l], moe_wr_e[l], moe_br_e[l])
        haug, route, counts = _cross_call(h, kt, vm, mem_wq[l].astype(BF16), mem_wo[l].astype(BF16),
                                          ln2_g[l], ln2_b[l], wr, br, seq)
        h = _moe_full(haug, route, counts, l, moe_w_gate, moe_w_up, moe_w_down, ln3_g[l], ln3_b[l])
    return h.reshape(bsz, seq, d)
```

```python
import functools
import math

import jax
import jax.numpy as jnp
from jax import lax
from jax.experimental import pallas as pl
from jax.experimental.pallas import tpu as pltpu

F32 = jnp.float32
BF16 = jnp.bfloat16

D_ATTN = 512
D_SSM = 256
D_LRU = 256
SSM_GROUPS = 16
SSM_GROUP = 16
SSM_STATE = 64
D_STATE = SSM_GROUPS * SSM_STATE
LRU_HEADS = 4
CONV_WIDTH = 4
LRU_C = 8.0
ATTN_HEADS = 8
ATTN_HEAD_DIM = 64
HEADS_PER_STEP = 4
MOBA_BLOCK = 256
MOBA_TOPK = 3
MEM_HEADS = 4
N_GROUPS = 4
EXPERTS_PER_GROUP = 4
N_EXPERTS = 16
DEPTH = 2
ALPHA = (2.0 * DEPTH) ** 0.25
LN_EPS = 1e-5
RMS_EPS = 1e-6
NEG_INF = -1e30

SUBLANES = 8
LANES = 128
VMEM_LIMIT = 48 * 1024 * 1024
ROW_TILE = 512
MOE_TILE = 1024
PERM_ROWS = 2048
PERM_UNROLL = 16
ROUTE_GROUP_LANE = 0
ROUTE_RANK_LANE = 1
ROUTE_WEIGHT_LANE0 = 8
PAIR_VALID, PAIR_FIRST, PAIR_LAST = 1, 2, 4
SCAN_CHUNK = 512


def _params(*sem):
    return pltpu.CompilerParams(dimension_semantics=sem, vmem_limit_bytes=VMEM_LIMIT)


def _const_spec(shape):
    zeros = (0,) * len(shape)
    return pl.BlockSpec(shape, lambda *_: zeros)


def _layer_norm(x, g, b):
    mu = jnp.mean(x, axis=-1, keepdims=True)
    xc = x - mu
    var = jnp.mean(xc * xc, axis=-1, keepdims=True)
    return xc * lax.rsqrt(var + LN_EPS) * g + b


def _rms(x):
    return x * lax.rsqrt(jnp.mean(x * x, axis=-1, keepdims=True) + RMS_EPS)


def _dot(a, b):
    return jnp.dot(a, b, preferred_element_type=F32)


def _dot_nt(a, b):
    return lax.dot_general(a, b, (((1,), (1,)), ((), ())), preferred_element_type=F32)


def _split_bf16(x):
    hi = x.astype(BF16)
    lo = (x - hi.astype(F32)).astype(BF16)
    return hi, lo


_IN_SPLITS = (D_ATTN, D_ATTN, D_ATTN, D_SSM, D_LRU, D_LRU)


def _project(hb, w_ref, out_refs):
    start = 0
    for o_ref, width in zip(out_refs, _IN_SPLITS):
        o_ref[...] = _dot(hb, w_ref[:, start:start + width])
        start += width


def _in_proj_kernel(h_ref, w_ref, *out_refs):
    _project(h_ref[...].astype(BF16), w_ref, out_refs)


def _ln_in_proj_kernel(x_ref, g_ref, b_ref, w_ref, h_ref, *out_refs):
    h = _layer_norm(x_ref[...], g_ref[...], b_ref[...])
    h_ref[...] = h
    _project(h.astype(BF16), w_ref, out_refs)


def _in_proj_call(h, w_in, ln=None):
    n, d = h.shape
    row = lambda w: pl.BlockSpec((ROW_TILE, w), lambda i: (i, 0))
    widths = _IN_SPLITS if ln is None else (d,) + _IN_SPLITS
    vecs = [] if ln is None else [v.reshape(1, d) for v in ln]
    return pl.pallas_call(
        _in_proj_kernel if ln is None else _ln_in_proj_kernel, grid=(n // ROW_TILE,),
        in_specs=[row(d)] + [_const_spec((1, d))] * len(vecs) + [_const_spec(w_in.shape)],
        out_specs=[row(w) for w in widths],
        out_shape=[jax.ShapeDtypeStruct((n, w), F32) for w in widths],
        compiler_params=_params("parallel"), name="in_proj")(h, *vecs, w_in)


AUG_ROWS = 16
ALIBI_LANE0 = SUBLANES
GATE_STRIDE = 32


def _moba_prepare(k_ref, v_ref, kt_scr, vb_scr, kmh_scr, kml_scr, nb):
    blk = MOBA_BLOCK
    width = k_ref.shape[1]
    arow = lax.broadcasted_iota(jnp.int32, (AUG_ROWS, blk), 0)
    akey = lax.broadcasted_iota(jnp.int32, (AUG_ROWS, blk), 1).astype(F32)
    lane_head = lax.shift_right_logical(lax.broadcasted_iota(jnp.int32, (1, width), 1),
                                        int(math.log2(ATTN_HEAD_DIM)))
    gate_rows = []
    for j in range(nb):
        kj = k_ref[j * blk:(j + 1) * blk, :]
        ktj = kj.T.astype(BF16)
        aug = jnp.where(arow == ALIBI_LANE0, float(blk * j),
                        jnp.where(arow == ALIBI_LANE0 + 1, akey, jnp.where(arow == j, 1.0, 0.0)))
        aug = aug.astype(BF16)
        kt_scr[0, j] = ktj
        kt_scr[0, j, LANES:LANES + AUG_ROWS, :] = aug
        kt_scr[1, j] = ktj
        kt_scr[1, j, 0:AUG_ROWS, :] = aug
        vj = v_ref[j * blk:(j + 1) * blk, :]
        vlane = lax.broadcasted_iota(jnp.int32, vj.shape, 1)
        vb_scr[0, j * blk:(j + 1) * blk, :] = jnp.where(vlane == LANES, 1.0, vj).astype(BF16)
        vb_scr[1, j * blk:(j + 1) * blk, :] = jnp.where(vlane == 0, 1.0, vj).astype(BF16)
        gate_rows.append(jnp.mean(kj, axis=0, keepdims=True))
    gate_rows.append(jnp.zeros((GATE_STRIDE - nb, width), F32))
    kmean = jnp.concatenate(gate_rows, axis=0)
    table = jnp.concatenate([jnp.where(lane_head == h, kmean, 0.0) for h in range(HEADS_PER_STEP)], axis=0)
    hi, lo = _split_bf16(table)
    kmh_scr[...] = hi
    kml_scr[...] = lo


def _moba_block(c, hg, slopes_ref, q_ref, o_ref, kt_scr, vb_scr, kmh_scr, kml_scr, nb):
    blk = MOBA_BLOCK
    width = q_ref.shape[1]
    q = q_ref[...]
    lane_head = lax.shift_right_logical(lax.broadcasted_iota(jnp.int32, (blk, width), 1),
                                        int(math.log2(ATTN_HEAD_DIM)))
    col = lax.broadcasted_iota(jnp.int32, (blk, LANES), 1)
    is_alibi = (col >= ALIBI_LANE0) & (col < ALIBI_LANE0 + 2)
    causal = (lax.broadcasted_iota(jnp.int32, (blk, blk), 0)
              >= lax.broadcasted_iota(jnp.int32, (blk, blk), 1))
    zeros = jnp.zeros((blk, LANES), F32)
    if c > MOBA_TOPK:
        qhi, qlo = _split_bf16(q)
        gate_t = (_dot_nt(kmh_scr[...], qhi) + _dot_nt(kml_scr[...], qhi)
                  + _dot_nt(kmh_scr[...], qlo))
        jrow = lax.broadcasted_iota(jnp.int32, (SUBLANES, blk), 0)
        pieces = []
        for hh in range(HEADS_PER_STEP):
            gate = jnp.where(jrow < c, gate_t[GATE_STRIDE * hh:GATE_STRIDE * hh + SUBLANES, :], NEG_INF)
            rank = jnp.zeros((SUBLANES, blk), jnp.int32)
            for jp in range(c):
                gj = jnp.broadcast_to(gate[jp:jp + 1, :], (SUBLANES, blk))
                beats = (gj > gate) | ((gj == gate) & (jrow > jp))
                rank = rank + beats.astype(jnp.int32)
            pieces.append(jnp.where((jrow < c) & (rank >= MOBA_TOPK), NEG_INF, 0.0))
            pieces.append(jnp.zeros((GATE_STRIDE - SUBLANES, blk), F32))
        penalty_all = jnp.concatenate(pieces, axis=0).T
    out = jnp.zeros((blk, width), F32)

    def score_tiles(hh):
        mine = lane_head == hh
        slope = slopes_ref[hg * HEADS_PER_STEP + hh]
        extra = jnp.where(is_alibi, slope, 0.0)
        if c > MOBA_TOPK:
            penalty = penalty_all if hh == 0 else pltpu.roll(penalty_all, LANES - GATE_STRIDE * hh, 1)
            extra = jnp.where(col < SUBLANES, penalty, extra)
        extra = jnp.concatenate([zeros, extra] if hh < HEADS_PER_STEP // 2 else [extra, zeros], axis=1)
        qa = jnp.where(mine, q * ATTN_HEAD_DIM ** -0.5, extra).astype(BF16)
        var = 0 if hh < HEADS_PER_STEP // 2 else 1
        tiles = [_dot(qa, kt_scr[var, j]) for j in range(c)]
        tiles.append(jnp.where(causal, _dot(qa, kt_scr[var, c]), NEG_INF))
        return tiles

    head_tiles = [score_tiles(hh) for hh in range(HEADS_PER_STEP)]
    head_max = []
    for tiles in head_tiles:
        mx = tiles[0]
        for t in tiles[1:]:
            mx = jnp.maximum(mx, t)
        head_max.append(jnp.max(mx, axis=-1, keepdims=True))
    for hh in range(HEADS_PER_STEP):
        mine = lane_head == hh
        var = 0 if hh < HEADS_PER_STEP // 2 else 1
        tiles = head_tiles[hh]
        m = head_max[hh]
        ctx = None
        for j, t in enumerate(tiles):
            part = _dot(jnp.exp(t - m).astype(BF16), vb_scr[var, j * blk:(j + 1) * blk, :])
            ctx = part if ctx is None else ctx + part
        ones_lane = LANES if var == 0 else 0
        out = jnp.where(mine, ctx / ctx[:, ones_lane:ones_lane + 1], out)
    o_ref[...] = out


def _moba_kernel(slopes_ref, q_ref, k_ref, v_ref, o_ref, kt_scr, vb_scr, kmh_scr, kml_scr):
    hg = pl.program_id(1)
    i = pl.program_id(2)
    nb = vb_scr.shape[1] // MOBA_BLOCK

    @pl.when(i == 0)
    def _():
        _moba_prepare(k_ref, v_ref, kt_scr, vb_scr, kmh_scr, kml_scr, nb)

    for c in range(nb):
        @pl.when(i == c)
        def _():
            _moba_block(c, hg, slopes_ref, q_ref, o_ref, kt_scr, vb_scr, kmh_scr, kml_scr, nb)


def _moba_call(q, k, v, bsz, seq):
    nb = seq // MOBA_BLOCK
    width = HEADS_PER_STEP * ATTN_HEAD_DIM
    slopes = jnp.asarray([2.0 ** (-8.0 * (h + 1) / ATTN_HEADS) for h in range(ATTN_HEADS)], F32)
    qspec = pl.BlockSpec((MOBA_BLOCK, width), lambda b, g, i: (b * nb + i, g))
    kvspec = pl.BlockSpec((seq, width), lambda b, g, i: (b, g))
    return pl.pallas_call(
        _moba_kernel, grid=(bsz, D_ATTN // width, nb),
        in_specs=[pl.BlockSpec(memory_space=pltpu.SMEM), qspec, kvspec, kvspec],
        out_specs=qspec, out_shape=jax.ShapeDtypeStruct(q.shape, F32),
        scratch_shapes=[pltpu.VMEM((2, nb, width, MOBA_BLOCK), BF16),
                        pltpu.VMEM((2, seq, width), BF16),
                        pltpu.VMEM((LANES, width), BF16),
                        pltpu.VMEM((LANES, width), BF16)],
        compiler_params=_params("parallel", "parallel", "arbitrary"), name="moba")(slopes, q, k, v)


def _s5_kernel(u_ref, bmat_ref, apow_ref, cmat_ref, d_ref, wglu_ref, bglu_ref, o_ref,
               half_scr, up_scr, x_scr, xb_scr, carry_scr):
    c = pl.program_id(1)
    chunk = u_ref.shape[0]
    steps = chunk // SUBLANES
    ns = D_STATE

    @pl.when(c == 0)
    def _():
        carry_scr[...] = jnp.zeros_like(carry_scr)

    halves = [slice(k * LANES, (k + 1) * LANES) for k in range(u_ref.shape[1] // LANES)]
    for k, cols in enumerate(halves):
        half_scr[k] = u_ref[:, cols]
        for tau in range(steps):
            up_scr[tau * SUBLANES:(tau + 1) * SUBLANES, cols] = (
                half_scr[k, pl.ds(tau, SUBLANES, stride=steps), :])
    up = up_scr[...]
    x_scr[...] = _dot(up.astype(BF16), bmat_ref[...])

    a1r = jnp.broadcast_to(apow_ref[0:1, 0:ns], (SUBLANES, ns))
    a1i = jnp.broadcast_to(apow_ref[0:1, ns:2 * ns], (SUBLANES, ns))

    def local_step(tau, state):
        xr, xi = state
        rows = pl.ds(pl.multiple_of(tau * SUBLANES, SUBLANES), SUBLANES)
        xr, xi = (a1r * xr - a1i * xi + x_scr[rows, 0:ns],
                  a1r * xi + a1i * xr + x_scr[rows, ns:2 * ns])
        x_scr[rows, 0:ns] = xr
        x_scr[rows, ns:2 * ns] = xi
        return xr, xi

    zero = jnp.zeros((SUBLANES, ns), F32)
    end_r, end_i = lax.fori_loop(0, steps, local_step, (zero, zero), unroll=4)

    alr = apow_ref[steps - 1:steps, 0:ns]
    ali = apow_ref[steps - 1:steps, ns:2 * ns]
    er = carry_scr[:, 0:ns]
    ei = carry_scr[:, ns:2 * ns]
    enter_r, enter_i = [], []
    for s in range(SUBLANES):
        enter_r.append(er)
        enter_i.append(ei)
        er, ei = (end_r[s:s + 1, :] + alr * er - ali * ei,
                  end_i[s:s + 1, :] + alr * ei + ali * er)
    carry_scr[:, 0:ns] = er
    carry_scr[:, ns:2 * ns] = ei
    cr = jnp.concatenate(enter_r, axis=0)
    ci = jnp.concatenate(enter_i, axis=0)

    def carry_step(pair, _):
        halves_r, halves_i = [], []
        for k in range(2):
            tau = 2 * pair + k
            rows = pl.ds(pl.multiple_of(tau * SUBLANES, SUBLANES), SUBLANES)
            pr = jnp.broadcast_to(apow_ref[pl.ds(tau, 1), 0:ns], (SUBLANES, ns))
            pi = jnp.broadcast_to(apow_ref[pl.ds(tau, 1), ns:2 * ns], (SUBLANES, ns))
            halves_r.append(x_scr[rows, 0:ns] + pr * cr - pi * ci)
            halves_i.append(x_scr[rows, ns:2 * ns] + pr * ci + pi * cr)
        rows2 = pl.ds(pl.multiple_of(pair * 2 * SUBLANES, 2 * SUBLANES), 2 * SUBLANES)
        xb_scr[rows2, 0:ns] = jnp.concatenate(halves_r, axis=0).astype(BF16)
        xb_scr[rows2, ns:2 * ns] = jnp.concatenate(halves_i, axis=0).astype(BF16)
        return 0

    lax.fori_loop(0, steps // 2, carry_step, 0, unroll=2)

    y = _dot(xb_scr[...], cmat_ref[...])
    y = jax.nn.gelu(y + d_ref[...] * up)
    gate = jax.nn.sigmoid(_dot(y.astype(BF16), wglu_ref[...]) + bglu_ref[...])
    up_scr[...] = y * gate
    for k, cols in enumerate(halves):
        for tau in range(steps):
            half_scr[k, pl.ds(tau, SUBLANES, stride=steps), :] = (
                up_scr[tau * SUBLANES:(tau + 1) * SUBLANES, cols])
        o_ref[:, cols] = half_scr[k]


def _complex_mul(ar, ai, br, bi):
    return ar * br - ai * bi, ar * bi + ai * br


def _s5_tables(a_re, a_im, b_re, b_im, c_re, c_im, log_dt):
    g, p, h = b_re.shape
    dt = jnp.exp(log_dt)[:, None]
    mag = jnp.exp(dt * a_re)
    ab_re, ab_im = mag * jnp.cos(dt * a_im), mag * jnp.sin(dt * a_im)
    den = a_re * a_re + a_im * a_im
    nr, ni = ab_re - 1.0, ab_im
    f_re = (nr * a_re + ni * a_im) / den
    f_im = (ni * a_re - nr * a_im) / den
    bb_re = f_re[..., None] * b_re - f_im[..., None] * b_im
    bb_im = f_re[..., None] * b_im + f_im[..., None] * b_re
    eye = jnp.eye(g, dtype=F32)
    bmat_re = jnp.einsum('gph,gk->ghkp', bb_re, eye).reshape(g * h, g * p)
    bmat_im = jnp.einsum('gph,gk->ghkp', bb_im, eye).reshape(g * h, g * p)
    bmat = jnp.concatenate([bmat_re, bmat_im], axis=1).astype(BF16)
    cmat_re = jnp.einsum('ghp,gk->gpkh', c_re, eye).reshape(g * p, g * h)
    cmat_im = jnp.einsum('ghp,gk->gpkh', c_im, eye).reshape(g * p, g * h)
    cmat = jnp.concatenate([cmat_re, -cmat_im], axis=0).astype(BF16)
    pr, pi = ab_re.reshape(1, g * p), ab_im.reshape(1, g * p)
    steps = SCAN_CHUNK // SUBLANES
    while pr.shape[0] < steps:
        nr2, ni2 = _complex_mul(pr, pi, pr[-1:], pi[-1:])
        pr, pi = jnp.concatenate([pr, nr2], axis=0), jnp.concatenate([pi, ni2], axis=0)
    apow = jnp.concatenate([pr[:steps], pi[:steps]], axis=1)
    return bmat, apow, cmat


def _s5_call(u, tables, d_skip, w_glu, b_glu, bsz, seq):
    bmat, apow, cmat = tables
    n, d = u.shape
    nc = seq // SCAN_CHUNK
    row = pl.BlockSpec((SCAN_CHUNK, d), lambda b, c: (b * nc + c, 0))
    return pl.pallas_call(
        _s5_kernel, grid=(bsz, nc),
        in_specs=[row, _const_spec(bmat.shape), _const_spec(apow.shape),
                  _const_spec(cmat.shape), _const_spec((1, d)), _const_spec((d, d)),
                  _const_spec((1, d))],
        out_specs=row, out_shape=jax.ShapeDtypeStruct((n, d), F32),
        scratch_shapes=[pltpu.VMEM((d // LANES, SCAN_CHUNK, LANES), F32),
                        pltpu.VMEM((SCAN_CHUNK, d), F32),
                        pltpu.VMEM((SCAN_CHUNK, 2 * D_STATE), F32),
                        pltpu.VMEM((SCAN_CHUNK, 2 * D_STATE), BF16),
                        pltpu.VMEM((1, 2 * D_STATE), F32)],
        compiler_params=_params("parallel", "arbitrary"), name="s5")(
            u, bmat, apow, cmat, d_skip.reshape(1, d), w_glu.astype(BF16), b_glu.reshape(1, d))


def _lru_kernel(xl_ref, gl_ref, cw_ref, cb_ref, wa_ref, ba_ref, wx_ref, bx_ref, lam_ref, o_ref,
                ext_scr, a_scr, h_scr, carry_scr):
    c = pl.program_id(1)
    chunk = xl_ref.shape[0]
    halo = SUBLANES

    @pl.when(c == 0)
    def _():
        ext_scr[0:halo, :] = jnp.zeros((halo, ext_scr.shape[1]), F32)
        carry_scr[...] = jnp.zeros_like(carry_scr)

    xl = xl_ref[...]
    ext_scr[halo:, :] = xl
    ext = ext_scr[...]
    xc = cb_ref[...] + cw_ref[CONV_WIDTH - 1:CONV_WIDTH, :] * xl
    for back in range(1, CONV_WIDTH):
        shifted = pltpu.roll(ext, back, 0)[halo:, :]
        xc = xc + cw_ref[CONV_WIDTH - 1 - back:CONV_WIDTH - back, :] * shifted
    ext_scr[0:halo, :] = xl[chunk - halo:, :]

    xb = xc.astype(BF16)
    r = jax.nn.sigmoid(_dot(xb, wa_ref[...]) + ba_ref[...])
    gate_in = jax.nn.sigmoid(_dot(xb, wx_ref[...]) + bx_ref[...])
    neg_lam = -lam_ref[...]
    softplus = jnp.maximum(neg_lam, 0.0) + jnp.log1p(jnp.exp(-jnp.abs(neg_lam)))
    log_a = -LRU_C * r * softplus
    a = jnp.exp(log_a)
    one_minus_a2 = jnp.tanh(-log_a) * (1.0 + a * a)
    a_scr[...] = a
    h_scr[...] = jnp.sqrt(one_minus_a2) * (gate_in * xc)

    sub = lax.broadcasted_iota(jnp.int32, (SUBLANES, a.shape[1]), 0)

    def tile_body(t, carry):
        rows = pl.ds(pl.multiple_of(t * SUBLANES, SUBLANES), SUBLANES)
        at = a_scr[rows, :]
        bt = h_scr[rows, :]
        for lvl in range(3):
            keep = sub >= (1 << lvl)
            a_sh = jnp.where(keep, pltpu.roll(at, 1 << lvl, 0), 1.0)
            b_sh = jnp.where(keep, pltpu.roll(bt, 1 << lvl, 0), 0.0)
            bt = bt + at * b_sh
            at = at * a_sh
        ht = bt + at * carry
        h_scr[rows, :] = ht
        return ht[SUBLANES - 1:SUBLANES, :]

    carry_scr[...] = lax.fori_loop(0, chunk // SUBLANES, tile_body, carry_scr[...], unroll=2)
    o_ref[...] = h_scr[...] * jax.nn.gelu(gl_ref[...])


def _block_diag(w):
    heads, di, do = w.shape
    eye = jnp.eye(heads, dtype=w.dtype)
    return jnp.einsum('hij,hk->hikj', w, eye).reshape(heads * di, heads * do)


def _lru_call(xl, gl, conv_w, conv_b, w_a, b_a, w_x, b_x, lam, bsz, seq):
    n, d = xl.shape
    nc = seq // SCAN_CHUNK
    row = pl.BlockSpec((SCAN_CHUNK, d), lambda b, c: (b * nc + c, 0))
    vec = _const_spec((1, d))
    mat = _const_spec((d, d))
    return pl.pallas_call(
        _lru_kernel, grid=(bsz, nc),
        in_specs=[row, row, _const_spec((CONV_WIDTH, d)), vec, mat, vec, mat, vec, vec],
        out_specs=row, out_shape=jax.ShapeDtypeStruct((n, d), F32),
        scratch_shapes=[pltpu.VMEM((SCAN_CHUNK + SUBLANES, d), F32),
                        pltpu.VMEM((SCAN_CHUNK, d), F32),
                        pltpu.VMEM((SCAN_CHUNK, d), F32),
                        pltpu.VMEM((1, d), F32)],
        compiler_params=_params("parallel", "arbitrary"), name="rglru")(
            xl, gl, conv_w, conv_b.reshape(1, d), _block_diag(w_a).astype(BF16), b_a.reshape(1, d),
            _block_diag(w_x).astype(BF16), b_x.reshape(1, d), lam.reshape(1, d))


def _out_proj_kernel(ya_ref, ys_ref, yl_ref, h_ref, mg_ref, w_ref, g_ref, b_ref, o_ref):
    mix = None
    start = 0
    for y_ref in (ya_ref, ys_ref, yl_ref):
        width = y_ref.shape[1]
        y = (_rms(y_ref[...]) * mg_ref[:, start:start + width]).astype(BF16)
        part = _dot(y, w_ref[start:start + width, :])
        mix = part if mix is None else mix + part
        start += width
    o_ref[...] = _layer_norm(ALPHA * h_ref[...] + mix, g_ref[...], b_ref[...])


def _out_proj_call(ya, ys, yl, h, mix_g, w_out, g, b):
    n, d = h.shape
    row = lambda w: pl.BlockSpec((ROW_TILE, w), lambda i: (i, 0))
    vec = _const_spec((1, d))
    return pl.pallas_call(
        _out_proj_kernel, grid=(n // ROW_TILE,),
        in_specs=[row(ya.shape[1]), row(ys.shape[1]), row(yl.shape[1]), row(d), vec,
                  _const_spec(w_out.shape), vec, vec],
        out_specs=row(d), out_shape=jax.ShapeDtypeStruct((n, d), F32),
        compiler_params=_params("parallel"), name="out_proj")(
            ya, ys, yl, h, mix_g.reshape(1, d), w_out, g.reshape(1, d), b.reshape(1, d))


def _mem_kv_kernel(mem_ref, wk_ref, wv_ref, kt_ref, v_ref):
    mb = mem_ref[...].astype(BF16)
    kt_ref[0] = _dot(mb, wk_ref[...]).T.astype(BF16)
    v_ref[0] = _dot(mb, wv_ref[...]).astype(BF16)


def _mem_kv_call(mem, wk, wv):
    bsz, m, d = mem.shape
    return pl.pallas_call(
        _mem_kv_kernel, grid=(bsz,),
        in_specs=[pl.BlockSpec((m, d), lambda b: (b, 0)), _const_spec((d, d)), _const_spec((d, d))],
        out_specs=[pl.BlockSpec((1, d, m), lambda b: (b, 0, 0)),
                   pl.BlockSpec((1, m, d), lambda b: (b, 0, 0))],
        out_shape=[jax.ShapeDtypeStruct((bsz, d, m), BF16), jax.ShapeDtypeStruct((bsz, m, d), BF16)],
        compiler_params=_params("parallel"), name="mem_kv")(mem.reshape(bsz * m, d), wk, wv)


def _cross_kernel(h_ref, kt_ref, v_ref, wq_ref, wo_ref, g_ref, b_ref, wr_ref, br_ref, upper_ref,
                  haug_ref, route_ref, count_ref, ctx_scr, run_scr):
    @pl.when(pl.program_id(0) == 0)
    def _():
        run_scr[...] = jnp.zeros_like(run_scr)

    h = h_ref[...]
    d = h.shape[1]
    hd = d // MEM_HEADS
    q = _dot(h.astype(BF16), wq_ref[...])
    scale = hd ** -0.5
    for hh in range(MEM_HEADS):
        cols = slice(hh * hd, (hh + 1) * hd)
        s = _dot(q[:, cols].astype(BF16), kt_ref[0, cols, :]) * scale
        s = s - jnp.max(s, axis=-1, keepdims=True)
        p = jnp.exp(s)
        p = p / jnp.sum(p, axis=-1, keepdims=True)
        ctx_scr[:, cols] = _dot(p.astype(BF16), v_ref[0, :, cols]).astype(BF16)
    cross = _dot(ctx_scr[...], wo_ref[...])
    h2 = _layer_norm(ALPHA * h + cross, g_ref[...], b_ref[...])

    rows = h2.shape[0]
    head, weights = _route(h2, wr_ref, br_ref, upper_ref, run_scr)
    record = jnp.concatenate([head] + weights + [jnp.zeros((LANES - 3 * SUBLANES, rows), F32)], axis=0)
    haug_ref[:, 0:d] = h2
    haug_ref[:, d:d + LANES] = record.T
    route_ref[...] = head
    count_ref[...] = run_scr[...]


def _cross_call(h, kt, v, wq, wo, g, b, wr, br, seq):
    n, d = h.shape
    m = v.shape[1]
    tiles_per_seq = seq // ROW_TILE
    row = lambda w: pl.BlockSpec((ROW_TILE, w), lambda i: (i, 0))
    vec = _const_spec((1, d))
    upper = jnp.triu(jnp.ones((ROW_TILE, ROW_TILE), BF16))
    return pl.pallas_call(
        _cross_kernel, grid=(n // ROW_TILE,),
        in_specs=[row(d), pl.BlockSpec((1, d, m), lambda i: (i // tiles_per_seq, 0, 0)),
                  pl.BlockSpec((1, m, d), lambda i: (i // tiles_per_seq, 0, 0)),
                  _const_spec((d, d)), _const_spec((d, d)), vec, vec,
                  _const_spec(wr.shape), _const_spec(br.shape), _const_spec(upper.shape)],
        out_specs=[row(d + LANES), pl.BlockSpec((SUBLANES, ROW_TILE), lambda i: (0, i)),
                   _const_spec((SUBLANES, LANES))],
        out_shape=[jax.ShapeDtypeStruct((n, d + LANES), F32), jax.ShapeDtypeStruct((SUBLANES, n), F32),
                   jax.ShapeDtypeStruct((SUBLANES, LANES), F32)],
        scratch_shapes=[pltpu.VMEM((ROW_TILE, d), BF16), pltpu.VMEM((SUBLANES, LANES), F32)],
        compiler_params=_params("arbitrary"), name="cross_attn")(
            h, kt, v, wq, wo, g.reshape(1, d), b.reshape(1, d), wr, br, upper)


def _route(h, wr_ref, br_ref, upper_ref, run_scr):
    rows = h.shape[0]
    hhi, hlo = _split_bf16(h)
    whi, wlo = _split_bf16(wr_ref[...])
    logits = _dot_nt(whi, hhi) + _dot_nt(wlo, hhi) + _dot_nt(whi, hlo) + br_ref[...]
    sub = lax.broadcasted_iota(jnp.int32, (SUBLANES, rows), 0)

    def rows_max(tiles):
        return functools.reduce(jnp.maximum, [jnp.max(t, axis=0, keepdims=True) for t in tiles])

    def rows_min(tiles):
        return functools.reduce(jnp.minimum, [jnp.min(t, axis=0, keepdims=True) for t in tiles])

    lg = jnp.where(sub < N_GROUPS, logits[0:SUBLANES], -jnp.inf)
    g_max = rows_max([lg])
    g_idx = rows_min([jnp.where(lg == g_max, sub, SUBLANES)])
    g_w = 1.0 / jnp.sum(jnp.exp(lg - g_max), axis=0, keepdims=True)

    tiles = [logits[SUBLANES * (1 + t):SUBLANES * (2 + t)] for t in range(2)]
    expert = [sub + SUBLANES * t for t in range(2)]
    in_group = [lax.shift_right_logical(e, int(math.log2(EXPERTS_PER_GROUP))) == g_idx for e in expert]
    masked = [jnp.where(s, t, -jnp.inf) for s, t in zip(in_group, tiles)]
    e_max = rows_max(masked)
    e_exp = [jnp.exp(m - e_max) for m in masked]
    e_sum = functools.reduce(jnp.add, [jnp.sum(e, axis=0, keepdims=True) for e in e_exp])
    prob = [e / e_sum for e in e_exp]

    def first_max(valid):
        vals = [jnp.where(v, p, -jnp.inf) for v, p in zip(valid, prob)]
        mx = rows_max(vals)
        idx = rows_min([jnp.where(v & (x == mx), e, N_EXPERTS) for v, x, e in zip(valid, vals, expert)])
        return mx, idx

    p1, i1 = first_max(in_group)
    p2, i2 = first_max([s & (e != i1) for s, e in zip(in_group, expert)])
    denom = p1 + p2
    weights = [jnp.where(e == i1, p1 / denom, jnp.where(e == i2, p2 / denom, 0.0)) * g_w for e in expert]

    onehot = sub == g_idx
    seen = (_dot(jnp.where(onehot, 1.0, 0.0).astype(BF16), upper_ref[...])
            + jnp.broadcast_to(run_scr[:, 0:1], (SUBLANES, rows)))
    rank = jnp.sum(jnp.where(onehot, seen - 1.0, 0.0), axis=0, keepdims=True)
    run_scr[...] = jnp.broadcast_to(seen[:, rows - 1:rows], run_scr.shape)
    head = jnp.where(sub == ROUTE_GROUP_LANE, g_idx.astype(F32), jnp.where(sub == ROUTE_RANK_LANE, rank, 0.0))
    return head, weights


def _router_tables(wr_g, br_g, wr_e, br_e):
    d = wr_g.shape[0]
    gap = ROUTE_WEIGHT_LANE0 - N_GROUPS
    tail = 4 * SUBLANES - ROUTE_WEIGHT_LANE0 - N_EXPERTS
    wr = jnp.concatenate([wr_g.T, jnp.zeros((gap, d), F32), wr_e.T, jnp.zeros((tail, d), F32)], axis=0)
    br = jnp.concatenate([br_g, jnp.zeros((gap,), F32), br_e, jnp.zeros((tail,), F32)])
    return wr, jnp.broadcast_to(br[:, None], (br.shape[0], ROW_TILE))


def _permute_kernel(scatter, pos_ref, src_ref, dst_ref, sem):
    base = pl.program_id(0) * PERM_ROWS

    def row_copy(r):
        there = pl.ds(pos_ref[base + r], 1)
        here = pl.ds(r, 1)
        if scatter:
            return pltpu.make_async_copy(src_ref.at[here, :], dst_ref.at[there, :], sem)
        return pltpu.make_async_copy(src_ref.at[there, :], dst_ref.at[here, :], sem)

    for r in range(PERM_ROWS):
        row_copy(r).start(priority=r % 2)

    def wait(i, carry):
        for k in range(PERM_UNROLL):
            row_copy(i * PERM_UNROLL + k).wait()
        return carry

    lax.fori_loop(0, PERM_ROWS // PERM_UNROLL, wait, 0)


def _permute_call(src, pos, scatter, name):
    n, w = src.shape
    block = pl.BlockSpec((PERM_ROWS, w), lambda i, pos: (i, 0))
    hbm = pl.BlockSpec(memory_space=pl.ANY)
    grid_spec = pltpu.PrefetchScalarGridSpec(
        num_scalar_prefetch=1, grid=(n // PERM_ROWS,),
        in_specs=[block if scatter else hbm], out_specs=hbm if scatter else block,
        scratch_shapes=[pltpu.SemaphoreType.DMA(())])
    return pl.pallas_call(
        functools.partial(_permute_kernel, scatter), grid_spec=grid_spec,
        out_shape=jax.ShapeDtypeStruct((n, w), src.dtype),
        compiler_params=_params("arbitrary"), name=name)(pos, src)


def _moe_kernel(tile_ref, group_ref, flag_ref, x_ref, wg_ref, wu_ref, wd_ref, g_ref, b_ref, o_ref,
                hb_scr, acc_scr):
    p = pl.program_id(0)
    e = pl.program_id(1)
    d = o_ref.shape[1]
    flags = flag_ref[p]

    @pl.when(((flags & PAIR_FIRST) != 0) & (e == 0))
    def _():
        hb_scr[...] = x_ref[:, 0:d].astype(BF16)
        acc_scr[...] = jnp.zeros_like(acc_scr)

    @pl.when((flags & PAIR_VALID) != 0)
    def _():
        hb = hb_scr[...]
        he = jax.nn.silu(_dot(hb, wg_ref[0, 0].astype(BF16))) * _dot(hb, wu_ref[0, 0].astype(BF16))
        y = _dot(he.astype(BF16), wd_ref[0, 0].astype(BF16))
        route = x_ref[:, d:d + LANES]
        lane = lax.broadcasted_iota(jnp.int32, route.shape, 1)
        mine = lane == ROUTE_WEIGHT_LANE0 + group_ref[p] * EXPERTS_PER_GROUP + e
        weight = jnp.sum(jnp.where(mine, route, 0.0), axis=-1, keepdims=True)
        acc_scr[...] += weight * y

    @pl.when(((flags & PAIR_LAST) != 0) & (e == pl.num_programs(1) - 1))
    def _():
        o_ref[...] = _layer_norm(ALPHA * x_ref[:, 0:d] + acc_scr[...], g_ref[...], b_ref[...])


def _moe_call(xs, pair_tile, pair_group, pair_flags, layer, w_gate, w_up, w_down, g, b):
    n, daug = xs.shape
    d = daug - LANES
    de = w_gate.shape[-1]
    last_e = EXPERTS_PER_GROUP - 1

    def expert(p, e, tile, group, flags):
        return group[p] * EXPERTS_PER_GROUP + jnp.where((flags[p] & PAIR_VALID) != 0, e, last_e)

    vec = pl.BlockSpec((1, d), lambda p, e, tile, group, flags: (0, 0))
    grid_spec = pltpu.PrefetchScalarGridSpec(
        num_scalar_prefetch=3, grid=(pair_tile.shape[0], EXPERTS_PER_GROUP),
        in_specs=[pl.BlockSpec((MOE_TILE, daug), lambda p, e, tile, group, flags: (tile[p], 0)),
                  pl.BlockSpec((1, 1, d, de), lambda p, e, *s: (layer, expert(p, e, *s), 0, 0)),
                  pl.BlockSpec((1, 1, d, de), lambda p, e, *s: (layer, expert(p, e, *s), 0, 0)),
                  pl.BlockSpec((1, 1, de, d), lambda p, e, *s: (layer, expert(p, e, *s), 0, 0)), vec, vec],
        out_specs=pl.BlockSpec((MOE_TILE, d), lambda p, e, tile, group, flags: (tile[p], 0)),
        scratch_shapes=[pltpu.VMEM((MOE_TILE, d), BF16), pltpu.VMEM((MOE_TILE, d), F32)])
    return pl.pallas_call(
        _moe_kernel, grid_spec=grid_spec, out_shape=jax.ShapeDtypeStruct((n, d), F32),
        compiler_params=_params("arbitrary", "arbitrary"), name="moe")(
            pair_tile, pair_group, pair_flags, xs, w_gate, w_up, w_down, g.reshape(1, d), b.reshape(1, d))


def _moe_tables(route, counts, n):
    gid = route[ROUTE_GROUP_LANE].astype(jnp.int32)
    rank = route[ROUTE_RANK_LANE].astype(jnp.int32)
    cnt = counts[:N_GROUPS, 0].astype(jnp.int32)
    ends = jnp.cumsum(cnt)
    starts = ends - cnt
    pos = starts[gid] + rank
    tiles = n // MOE_TILE
    pairs = tiles + N_GROUPS - 1
    lo = jnp.arange(tiles, dtype=jnp.int32)[:, None] * MOE_TILE
    active = ((starts[None, :] < lo + MOE_TILE) & (ends[None, :] > lo)).reshape(-1)
    n_active = jnp.sum(active.astype(jnp.int32))
    idx = jnp.nonzero(active, size=pairs, fill_value=0)[0].astype(jnp.int32)
    valid = jnp.arange(pairs, dtype=jnp.int32) < n_active
    idx = jnp.where(valid, idx, idx[n_active - 1])
    tile = idx // N_GROUPS
    group = idx % N_GROUPS
    prev_tile = jnp.concatenate([jnp.full((1,), -1, jnp.int32), tile[:-1]])
    next_tile = jnp.concatenate([tile[1:], jnp.full((1,), -1, jnp.int32)])
    next_valid = jnp.concatenate([valid[1:], jnp.zeros((1,), bool)])
    first = valid & (tile != prev_tile)
    last = valid & ((tile != next_tile) | ~next_valid)
    flags = (valid.astype(jnp.int32) * PAIR_VALID + first.astype(jnp.int32) * PAIR_FIRST
             + last.astype(jnp.int32) * PAIR_LAST)
    return pos, tile, group, flags


def _moe_full(haug, route, counts, layer, w_gate, w_up, w_down, g, b):
    n = haug.shape[0]
    pos, tile, group, flags = _moe_tables(route, counts, n)
    xs = _permute_call(haug, pos, True, "moe_dispatch")
    ys = _moe_call(xs, tile, group, flags, layer, w_gate, w_up, w_down, g, b)
    return _permute_call(ys, pos, False, "moe_return")


def kernel(x, mem, ln0_g, ln0_b, w_in, mix_g, w_out, ssm_a_re, ssm_a_im, ssm_b_re, ssm_b_im, ssm_c_re, ssm_c_im, ssm_d, ssm_log_dt, ssm_w_glu, ssm_b_glu, lru_conv_w, lru_conv_b, lru_w_a, lru_b_a, lru_w_x, lru_b_x, lru_lam, ln1_g, ln1_b, mem_wq, mem_wk, mem_wv, mem_wo, ln2_g, ln2_b, moe_wr_g, moe_br_g, moe_wr_e, moe_br_e, moe_w_gate, moe_w_up, moe_w_down, ln3_g, ln3_b):
    bsz, seq, d = x.shape
    depth = w_in.shape[0]
    assert seq % MOBA_BLOCK == 0 and seq % SCAN_CHUNK == 0 and seq % ROW_TILE == 0
    assert seq // MOBA_BLOCK <= SUBLANES
    h = x.reshape(bsz * seq, d)
    for l in range(depth):
        if l == 0:
            h, q, k, v, u, xl, gl = _in_proj_call(h, w_in[l].astype(BF16), ln=(ln0_g, ln0_b))
        else:
            q, k, v, u, xl, gl = _in_proj_call(h, w_in[l].astype(BF16))
        y_attn = _moba_call(q, k, v, bsz, seq)
        tables = _s5_tables(ssm_a_re[l], ssm_a_im[l], ssm_b_re[l], ssm_b_im[l], ssm_c_re[l],
                            ssm_c_im[l], ssm_log_dt[l])
        y_ssm = _s5_call(u, tables, ssm_d[l], ssm_w_glu[l], ssm_b_glu[l], bsz, seq)
        y_lru = _lru_call(xl, gl, lru_conv_w[l], lru_conv_b[l], lru_w_a[l], lru_b_a[l],
                          lru_w_x[l], lru_b_x[l], lru_lam[l], bsz, seq)
        h = _out_proj_call(y_attn, y_ssm, y_lru, h, mix_g[l], w_out[l].astype(BF16),
                           ln1_g[l], ln1_b[l])
        kt, vm = _mem_kv_call(mem, mem_wk[l].astype(BF16), mem_wv[l].astype(BF16))
        wr, br = _router_tables(moe_wr_g[l], moe_br_g[l], moe_wr_e[l], moe_br_e[l])
        haug, route, counts = _cross_call(h, kt, vm, mem_wq[l].astype(BF16), mem_wo[l].astype(BF16),
                                          ln2_g[l], ln2_b[l], wr, br, seq)
        h = _moe_full(haug, route, counts, l, moe_w_gate, moe_w_up, moe_w_down, ln3_g[l], ln3_b[l])
    return h.reshape(bsz, seq, d)
```
